```python
import jax, jax.numpy as jnp
from jax import lax
import numpy as np

D_MODEL = 1024
BATCH = 4
SEQ = 8192
DEPTH = 4

CHUNK = 64
N_HEADS = 16
HEAD_DIM = D_MODEL // N_HEADS
D_FF = 4 * D_MODEL
N_MIXERS = 2
LEFT_CHUNKS = 8
BAND = (LEFT_CHUNKS + 1) * CHUNK
MAX_REL = 256
N_REL = 2 * MAX_REL + 1
Q_BLOCK = 128
RMS_EPS = 1e-6
N_SB_LAYERS = (DEPTH + N_MIXERS - 1) // N_MIXERS
N_CA_LAYERS = DEPTH // N_MIXERS

kernel_name = "hybrid_stickbreak_chunkrel_trunk"


def rms_norm(x, gain):
    x32 = x.astype(jnp.float32)
    y = x32 * lax.rsqrt(jnp.mean(x32 * x32, axis=-1, keepdims=True) + RMS_EPS)
    return (y * gain.astype(jnp.float32)).astype(x.dtype)


def split_heads(t):
    b, s, _ = t.shape
    return jnp.transpose(t.reshape(b, s, N_HEADS, HEAD_DIM), (0, 2, 1, 3))


def merge_heads(t):
    b, h, s, d = t.shape
    return jnp.transpose(t, (0, 2, 1, 3)).reshape(b, s, h * d)


def stick_breaking_attention(q, k, v):
    seq = q.shape[2]
    scale = HEAD_DIM ** -0.5
    outs = []
    for qb in range(seq // Q_BLOCK):
        q0 = qb * Q_BLOCK
        kl = q0 + Q_BLOCK
        qblk = q[:, :, q0:kl].astype(jnp.float32)
        kblk = k[:, :, :kl].astype(jnp.float32)
        vblk = v[:, :, :kl].astype(jnp.float32)
        z = jnp.einsum('bhqd,bhkd->bhqk', qblk, kblk) * scale
        t_pos = q0 + jnp.arange(Q_BLOCK)[:, None]
        s_pos = jnp.arange(kl)[None, :]
        valid = s_pos < t_pos
        log_1mb = jnp.where(valid, -jax.nn.softplus(z), 0.0)
        between = lax.cumsum(log_1mb, axis=3, reverse=True) - log_1mb
        log_a = jax.nn.log_sigmoid(z) + between
        a = jnp.where(valid, jnp.exp(log_a), 0.0)
        outs.append(jnp.einsum('bhqk,bhkd->bhqd', a, vblk))
    return jnp.concatenate(outs, axis=2).astype(q.dtype)


def head_rms_norm(t, gain):
    return rms_norm(t, gain)


def chunked_relpos_attention(q, k, v, rel_bias):
    b, h, seq, dh = q.shape
    n_chunks = seq // CHUNK
    pad = LEFT_CHUNKS * CHUNK
    kp = jnp.pad(k, ((0, 0), (0, 0), (pad, 0), (0, 0)))
    vp = jnp.pad(v, ((0, 0), (0, 0), (pad, 0), (0, 0)))
    qi = jnp.arange(CHUNK)[:, None]
    kj = jnp.arange(BAND)[None, :]
    rel_idx = jnp.clip(qi + pad - kj, -MAX_REL, MAX_REL) + MAX_REL
    bias = rel_bias[:, rel_idx].astype(jnp.float32)
    scale = dh ** -0.5

    def one_chunk(c):
        start = c * CHUNK
        qc = lax.dynamic_slice_in_dim(q, start, CHUNK, axis=2).astype(jnp.float32)
        kc = lax.dynamic_slice_in_dim(kp, start, BAND, axis=2).astype(jnp.float32)
        vc = lax.dynamic_slice_in_dim(vp, start, BAND, axis=2).astype(jnp.float32)
        logits = jnp.einsum('bhqd,bhkd->bhqk', qc, kc) * scale + bias[None]
        key_pos = start - pad + jnp.arange(BAND)
        logits = jnp.where((key_pos >= 0)[None, None, None, :], logits, -jnp.inf)
        p = jax.nn.softmax(logits, axis=-1)
        return jnp.einsum('bhqk,bhkd->bhqd', p, vc)

    out = lax.map(one_chunk, jnp.arange(n_chunks))
    out = jnp.transpose(out, (1, 2, 0, 3, 4)).reshape(b, h, seq, dh)
    return out.astype(q.dtype)


def setup_inputs(seed: int = 0) -> dict:
    key = jax.random.key(seed)
    ks = jax.random.split(key, 12)
    f32 = jnp.float32
    x = jax.random.normal(ks[0], (BATCH, SEQ, D_MODEL), f32)
    mix_norm = 1.0 + 0.02 * jax.random.normal(ks[1], (DEPTH, D_MODEL), f32)
    w_qkv = jax.random.normal(ks[2], (DEPTH, D_MODEL, 3 * D_MODEL), f32) * D_MODEL ** -0.5
    w_o = jax.random.normal(ks[3], (DEPTH, D_MODEL, D_MODEL), f32) * D_MODEL ** -0.5
    q_norm = 1.0 + 0.02 * jax.random.normal(ks[4], (N_CA_LAYERS, HEAD_DIM), f32)
    k_norm = 1.0 + 0.02 * jax.random.normal(ks[5], (N_CA_LAYERS, HEAD_DIM), f32)
    rel_bias = 0.1 * jax.random.normal(ks[6], (N_CA_LAYERS, N_HEADS, N_REL), f32)
    ffn_norm = 1.0 + 0.02 * jax.random.normal(ks[7], (DEPTH, D_MODEL), f32)
    w_up = jax.random.normal(ks[8], (DEPTH, D_MODEL, D_FF), f32) * D_MODEL ** -0.5
    w_down = jax.random.normal(ks[9], (DEPTH, D_FF, D_MODEL), f32) * D_FF ** -0.5
    return {"x": x, "mix_norm": mix_norm, "w_qkv": w_qkv, "w_o": w_o,
            "q_norm": q_norm, "k_norm": k_norm, "rel_bias": rel_bias,
            "ffn_norm": ffn_norm, "w_up": w_up, "w_down": w_down}


def reference(x, mix_norm, w_qkv, w_o, q_norm, k_norm, rel_bias, ffn_norm, w_up, w_down):
    for layer in range(DEPTH):
        h = rms_norm(x, mix_norm[layer])
        qkv = jnp.einsum('bsd,de->bse', h, w_qkv[layer])
        q, k, v = jnp.split(qkv, 3, axis=-1)
        q, k, v = split_heads(q), split_heads(k), split_heads(v)
        if layer % N_MIXERS == 0:
            o = stick_breaking_attention(q, k, v)
        else:
            idx = layer // N_MIXERS
            q = head_rms_norm(q, q_norm[idx])
            k = head_rms_norm(k, k_norm[idx])
            o = chunked_relpos_attention(q, k, v, rel_bias[idx])
        x = x + jnp.einsum('bsd,de->bse', merge_heads(o), w_o[layer])
        h = rms_norm(x, ffn_norm[layer])
        u = jnp.square(jax.nn.relu(jnp.einsum('bsd,df->bsf', h, w_up[layer])))
        x = x + jnp.einsum('bsf,fd->bsd', u, w_down[layer])
    return x
```

```python
import functools

import jax
import jax.numpy as jnp
from jax import lax
from jax.experimental import pallas as pl
from jax.experimental.pallas import tpu as pltpu

D_MODEL = 1024
N_HEADS = 16
HEAD_DIM = D_MODEL // N_HEADS
D_FF = 4 * D_MODEL
CHUNK = 64
LEFT_CHUNKS = 8
MAX_REL = 256
RMS_EPS = 1e-6

LANES = 128
HEADS_PER_TILE = LANES // HEAD_DIM
N_PAIRS = N_HEADS // HEADS_PER_TILE
ATTN_SCALE = HEAD_DIM ** -0.5

TOKEN_TILE = 512
FF_TILE = 1024
SB_TILE = 128
CA_TILE = 2 * CHUNK
CA_KEY_TILES = LEFT_CHUNKS * CHUNK // CA_TILE + 1
MASK_VALUE = -1e30

SB_LOG_CUTOFF = -110.0

VMEM_LIMIT = 56 * 1024 * 1024

f32 = jnp.float32
bf16 = jnp.bfloat16


def _rms_scale(x):
    return lax.rsqrt(jnp.mean(x * x, axis=-1, keepdims=True) + RMS_EPS)


def _split_bf16(x):
    hi = x.astype(bf16)
    lo = (x - hi.astype(f32)).astype(bf16)
    return hi, lo


def _dot(a, b):
    return jnp.dot(a, b, preferred_element_type=f32)


def _dot_nt(a, b):
    return lax.dot_general(a, b, (((1,), (1,)), ((), ())), preferred_element_type=f32)


def _norm_qkv_kernel(x_ref, g_ref, w_ref, qg_ref, kg_ref, out_ref, *, head_norm):
    x = x_ref[...]
    h = (x * _rms_scale(x) * g_ref[...]).astype(bf16)
    if head_norm:
        r = lax.broadcasted_iota(jnp.int32, (LANES, LANES), 0) // HEAD_DIM
        c = lax.broadcasted_iota(jnp.int32, (LANES, LANES), 1) // HEAD_DIM
        head_mean = jnp.where(r == c, 1.0 / HEAD_DIM, 0.0).astype(bf16)
    for part in range(3):
        cols = slice(part * D_MODEL, (part + 1) * D_MODEL)
        acc = _dot(h, w_ref[:, cols])
        if head_norm and part < 2:
            gain = (qg_ref if part == 0 else kg_ref)[...]
            for p in range(N_PAIRS):
                blk = acc[:, p * LANES:(p + 1) * LANES]
                hi, lo = _split_bf16(blk * blk)
                ms = _dot(hi, head_mean) + _dot(lo, head_mean)
                y = blk * lax.rsqrt(ms + RMS_EPS) * gain
                out_ref[:, part * D_MODEL + p * LANES:part * D_MODEL + (p + 1) * LANES] = y.astype(bf16)
        else:
            out_ref[:, cols] = acc.astype(bf16)


def _norm_qkv(x, gain, w, q_gain, k_gain, head_norm):
    n = x.shape[0]
    const = lambda i: (0, 0)
    return pl.pallas_call(
        functools.partial(_norm_qkv_kernel, head_norm=head_norm),
        name="norm_qkv_hn" if head_norm else "norm_qkv",
        grid=(n // TOKEN_TILE,),
        in_specs=[
            pl.BlockSpec((TOKEN_TILE, D_MODEL), lambda i: (i, 0)),
            pl.BlockSpec((1, D_MODEL), const),
            pl.BlockSpec((D_MODEL, 3 * D_MODEL), const, pipeline_mode=pl.Buffered(1)),
            pl.BlockSpec((1, LANES), const),
            pl.BlockSpec((1, LANES), const),
        ],
        out_specs=pl.BlockSpec((TOKEN_TILE, 3 * D_MODEL), lambda i: (i, 0)),
        out_shape=jax.ShapeDtypeStruct((n, 3 * D_MODEL), bf16),
        compiler_params=pltpu.CompilerParams(
            dimension_semantics=("parallel",), vmem_limit_bytes=VMEM_LIMIT),
    )(x, gain, w, q_gain, k_gain)


def _sb_attn_kernel(q_ref, k_ref, v_ref, o_ref):
    t = SB_TILE
    i = pl.program_id(2)
    q = q_ref[0]
    lane = lax.broadcasted_iota(jnp.int32, (t, LANES), 1)
    row = lax.broadcasted_iota(jnp.int32, (t, t), 0)
    col = lax.broadcasted_iota(jnp.int32, (t, t), 1)
    causal = col < row
    r2 = lax.broadcasted_iota(jnp.int32, (t, 2 * t), 0)
    c2 = lax.broadcasted_iota(jnp.int32, (t, 2 * t), 1)
    cum_mat = jnp.where((r2 > c2) | (c2 >= t), 1.0, 0.0).astype(bf16)

    def key_tile(qh, j, log_rest, acc, diagonal):
        start = pl.multiple_of(j * t, t)
        z = _dot_nt(qh, k_ref[0, pl.ds(start, t), :]) * ATTN_SCALE
        l = -(jnp.maximum(z, 0.0) + jnp.log(1.0 + jnp.exp(-jnp.abs(z))))
        if diagonal:
            l = jnp.where(causal, l, 0.0)
        hi, lo = _split_bf16(l)
        cum = _dot(hi, cum_mat) + _dot(lo, cum_mat)
        a = jnp.exp(z + l + cum[:, :t] + log_rest)
        if diagonal:
            a = jnp.where(causal, a, 0.0)
        acc = acc + _dot(a.astype(bf16), v_ref[0, pl.ds(start, t), :])
        return log_rest + cum[:, t:], acc

    outs = []
    for head in range(HEADS_PER_TILE):
        qh = jnp.where(lane // HEAD_DIM == head, q, jnp.zeros_like(q))
        zeros = jnp.zeros((t, t), f32)
        log_rest, acc = key_tile(qh, i, zeros, zeros, True)

        def cond(state):
            j, _, _, top = state
            return jnp.logical_and(j >= 0, top > SB_LOG_CUTOFF)

        def body(state, qh=qh):
            j, log_rest, acc, _ = state
            log_rest, acc = key_tile(qh, j, log_rest, acc, False)
            return j - 1, log_rest, acc, jnp.max(log_rest)

        _, _, acc, _ = lax.while_loop(cond, body, (i - 1, log_rest, acc, jnp.max(log_rest)))
        outs.append(acc)
    o_ref[0] = jnp.where(lane < HEAD_DIM, outs[0], outs[1]).astype(o_ref.dtype)


def _sb_attention(qkv):
    b, s, _ = qkv.shape
    return pl.pallas_call(
        _sb_attn_kernel,
        name="sb_attention",
        grid=(b, N_PAIRS, s // SB_TILE),
        in_specs=[
            pl.BlockSpec((1, SB_TILE, LANES), lambda bi, p, i: (bi, i, p)),
            pl.BlockSpec((1, s, LANES), lambda bi, p, i: (bi, 0, N_PAIRS + p)),
            pl.BlockSpec((1, s, LANES), lambda bi, p, i: (bi, 0, 2 * N_PAIRS + p)),
        ],
        out_specs=pl.BlockSpec((1, SB_TILE, LANES), lambda bi, p, i: (bi, i, p)),
        out_shape=jax.ShapeDtypeStruct((b, s, D_MODEL), bf16),
        compiler_params=pltpu.CompilerParams(
            dimension_semantics=("parallel", "parallel", "parallel"),
            vmem_limit_bytes=VMEM_LIMIT),
    )(qkv, qkv, qkv)


def _chunk_attn_kernel(q_ref, k_ref, v_ref, bias_ref, o_ref):
    t = CA_TILE
    i = pl.program_id(2)
    q = q_ref[0]
    lane = lax.broadcasted_iota(jnp.int32, (t, LANES), 1)
    outs = []
    for head in range(HEADS_PER_TILE):
        qh = jnp.where(lane // HEAD_DIM == head, q, jnp.zeros_like(q))
        logits, starts = [], []
        for kt in range(CA_KEY_TILES):
            j = i - (CA_KEY_TILES - 1) + kt
            start = pl.multiple_of(jnp.maximum(j, 0) * t, t)
            s = _dot_nt(qh, k_ref[0, pl.ds(start, t), :]) * ATTN_SCALE + bias_ref[head, kt]
            logits.append(s + jnp.where(j < 0, MASK_VALUE, 0.0))
            starts.append(start)
        top = functools.reduce(jnp.maximum, logits)
        top = jnp.max(top, axis=-1, keepdims=True)
        acc = jnp.zeros((t, LANES), f32)
        denom = jnp.zeros((t, t), f32)
        for s, start in zip(logits, starts):
            p = jnp.exp(s - top)
            denom = denom + p
            acc = acc + _dot(p.astype(bf16), v_ref[0, pl.ds(start, t), :])
        outs.append(acc / jnp.sum(denom, axis=-1, keepdims=True))
    o_ref[0] = jnp.where(lane < HEAD_DIM, outs[0], outs[1]).astype(o_ref.dtype)


def _chunk_bias_table(rel_bias):
    t = CA_TILE
    kt = jnp.arange(CA_KEY_TILES)[:, None, None]
    r = jnp.arange(t)[None, :, None]
    c = jnp.arange(t)[None, None, :]
    key_off = (kt - (CA_KEY_TILES - 1)) * t + c
    dist = r - key_off
    chunk_dist = r // CHUNK - jnp.floor_divide(key_off, CHUNK)
    in_band = (chunk_dist >= 0) & (chunk_dist <= LEFT_CHUNKS)
    idx = jnp.clip(dist, -MAX_REL, MAX_REL) + MAX_REL
    return jnp.where(in_band[None], rel_bias[:, idx].astype(f32), MASK_VALUE)


def _chunk_attention(qkv, rel_bias):
    b, s, _ = qkv.shape
    table = _chunk_bias_table(rel_bias)
    return pl.pallas_call(
        _chunk_attn_kernel,
        name="chunk_attention",
        grid=(b, N_PAIRS, s // CA_TILE),
        in_specs=[
            pl.BlockSpec((1, CA_TILE, LANES), lambda bi, p, i: (bi, i, p)),
            pl.BlockSpec((1, s, LANES), lambda bi, p, i: (bi, 0, N_PAIRS + p)),
            pl.BlockSpec((1, s, LANES), lambda bi, p, i: (bi, 0, 2 * N_PAIRS + p)),
            pl.BlockSpec((HEADS_PER_TILE, CA_KEY_TILES, CA_TILE, CA_TILE),
                         lambda bi, p, i: (p, 0, 0, 0)),
        ],
        out_specs=pl.BlockSpec((1, CA_TILE, LANES), lambda bi, p, i: (bi, i, p)),
        out_shape=jax.ShapeDtypeStruct((b, s, D_MODEL), bf16),
        compiler_params=pltpu.CompilerParams(
            dimension_semantics=("parallel", "parallel", "parallel"),
            vmem_limit_bytes=VMEM_LIMIT),
    )(qkv, qkv, qkv, table)


def _proj_mlp_kernel(x_ref, o_ref, wo_ref, g_ref, wup_ref, wdn_ref, out_ref):
    x = x_ref[...] + _dot(o_ref[...], wo_ref[...])
    h = (x * _rms_scale(x) * g_ref[...]).astype(bf16)
    acc = x
    for c in range(D_FF // FF_TILE):
        cols = slice(c * FF_TILE, (c + 1) * FF_TILE)
        u = jnp.square(jnp.maximum(_dot(h, wup_ref[:, cols]), 0.0))
        acc = acc + _dot(u.astype(bf16), wdn_ref[cols, :])
    out_ref[...] = acc


def _proj_mlp(x, o, wo, gain, wup, wdn):
    n = x.shape[0]
    const = lambda i: (0, 0)
    rows = lambda i: (i, 0)
    return pl.pallas_call(
        _proj_mlp_kernel,
        name="proj_mlp",
        grid=(n // TOKEN_TILE,),
        in_specs=[
            pl.BlockSpec((TOKEN_TILE, D_MODEL), rows),
            pl.BlockSpec((TOKEN_TILE, D_MODEL), rows),
            pl.BlockSpec((D_MODEL, D_MODEL), const, pipeline_mode=pl.Buffered(1)),
            pl.BlockSpec((1, D_MODEL), const),
            pl.BlockSpec((D_MODEL, D_FF), const, pipeline_mode=pl.Buffered(1)),
            pl.BlockSpec((D_FF, D_MODEL), const, pipeline_mode=pl.Buffered(1)),
        ],
        out_specs=pl.BlockSpec((TOKEN_TILE, D_MODEL), rows),
        out_shape=jax.ShapeDtypeStruct((n, D_MODEL), f32),
        compiler_params=pltpu.CompilerParams(
            dimension_semantics=("parallel",), vmem_limit_bytes=VMEM_LIMIT),
    )(x, o, wo, gain, wup, wdn)


def kernel(x, mix_norm, w_qkv, w_o, q_norm, k_norm, rel_bias, ffn_norm, w_up, w_down):
    b, s, d = x.shape
    depth = w_qkv.shape[0]
    x = x.reshape(b * s, d)
    unit_gain = jnp.ones((1, LANES), f32)
    for layer in range(depth):
        chunked = layer % 2 == 1
        idx = layer // 2
        if chunked:
            q_gain = jnp.tile(q_norm[idx], HEADS_PER_TILE)[None]
            k_gain = jnp.tile(k_norm[idx], HEADS_PER_TILE)[None]
        else:
            q_gain = k_gain = unit_gain
        qkv = _norm_qkv(x, mix_norm[layer][None], w_qkv[layer].astype(bf16),
                        q_gain, k_gain, chunked)
        qkv = qkv.reshape(b, s, 3 * d)
        o = _chunk_attention(qkv, rel_bias[idx]) if chunked else _sb_attention(qkv)
        x = _proj_mlp(x, o.reshape(b * s, d), w_o[layer].astype(bf16), ffn_norm[layer][None],
                      w_up[layer].astype(bf16), w_down[layer].astype(bf16))
    return x.reshape(b, s, d)
```

```python
import functools

import jax
import jax.numpy as jnp
from jax import lax
from jax.experimental import pallas as pl
from jax.experimental.pallas import tpu as pltpu

D_MODEL = 1024
N_HEADS = 16
HEAD_DIM = D_MODEL // N_HEADS
D_FF = 4 * D_MODEL
CHUNK = 64
LEFT_CHUNKS = 8
MAX_REL = 256
RMS_EPS = 1e-6

LANES = 128
HEADS_PER_TILE = LANES // HEAD_DIM
N_PAIRS = N_HEADS // HEADS_PER_TILE
ATTN_SCALE = HEAD_DIM ** -0.5

TOKEN_TILE = 512
FF_TILE = 1024
SB_TILE = 128
SB_STEP_TILES = 8
SB_FAST_TILES = 3
CA_TILE = 2 * CHUNK
CA_KEY_TILES = LEFT_CHUNKS * CHUNK // CA_TILE + 1
CA_STEP_TILES = 4
CA_BAND = CA_KEY_TILES * CA_TILE
CA_ROLL_WIDTH = 1024
MASK_VALUE = -1e30

SB_DECAY_CUTOFF = 110.0

VMEM_LIMIT = 56 * 1024 * 1024

f32 = jnp.float32
bf16 = jnp.bfloat16


def _rms_scale(x):
    return lax.rsqrt(jnp.mean(x * x, axis=-1, keepdims=True) + RMS_EPS)


def _split_bf16(x):
    hi = x.astype(bf16)
    lo = (x - hi.astype(f32)).astype(bf16)
    return hi, lo


def _dot(a, b):
    return jnp.dot(a, b, preferred_element_type=f32)


def _dot_nt(a, b):
    return lax.dot_general(a, b, (((1,), (1,)), ((), ())), preferred_element_type=f32)


def _split_heads_per_tile(k_ref, v_ref, kcat_ref, vcat_ref, t):
    lane = lax.broadcasted_iota(jnp.int32, (t, LANES), 1)

    def build(j, carry):
        rows = pl.ds(pl.multiple_of(j * t, t), t)
        k = k_ref[0, rows, :]
        v = v_ref[0, rows, :]
        for head in range(HEADS_PER_TILE):
            own = lane // HEAD_DIM == head
            kcat_ref[j, head * t:(head + 1) * t, :] = jnp.where(own, k, jnp.zeros_like(k))
            vcat_ref[j, head * t:(head + 1) * t, :] = jnp.where(own, v, jnp.zeros_like(v))
        return carry

    lax.fori_loop(0, k_ref.shape[1] // t, build, 0)


def _norm_qkv_kernel(x_ref, g_ref, w_ref, qg_ref, kg_ref, out_ref, *, head_norm):
    x = x_ref[...]
    h = (x * _rms_scale(x) * g_ref[...]).astype(bf16)
    if head_norm:
        r = lax.broadcasted_iota(jnp.int32, (LANES, LANES), 0) // HEAD_DIM
        c = lax.broadcasted_iota(jnp.int32, (LANES, LANES), 1) // HEAD_DIM
        head_mean = jnp.where(r == c, 1.0 / HEAD_DIM, 0.0).astype(bf16)
    for part in range(3):
        cols = slice(part * D_MODEL, (part + 1) * D_MODEL)
        acc = _dot(h, w_ref[:, cols])
        if head_norm and part < 2:
            gain = (qg_ref if part == 0 else kg_ref)[...]
            for p in range(N_PAIRS):
                blk = acc[:, p * LANES:(p + 1) * LANES]
                hi, lo = _split_bf16(blk * blk)
                ms = _dot(hi, head_mean) + _dot(lo, head_mean)
                y = blk * lax.rsqrt(ms + RMS_EPS) * gain
                out_ref[:, part * D_MODEL + p * LANES:part * D_MODEL + (p + 1) * LANES] = y.astype(bf16)
        else:
            out_ref[:, cols] = acc.astype(bf16)


def _norm_qkv(x, gain, w, q_gain, k_gain, head_norm):
    n = x.shape[0]
    const = lambda i: (0, 0)
    return pl.pallas_call(
        functools.partial(_norm_qkv_kernel, head_norm=head_norm),
        name="norm_qkv_hn" if head_norm else "norm_qkv",
        grid=(n // TOKEN_TILE,),
        in_specs=[
            pl.BlockSpec((TOKEN_TILE, D_MODEL), lambda i: (i, 0)),
            pl.BlockSpec((1, D_MODEL), const),
            pl.BlockSpec((D_MODEL, 3 * D_MODEL), const, pipeline_mode=pl.Buffered(1)),
            pl.BlockSpec((1, LANES), const),
            pl.BlockSpec((1, LANES), const),
        ],
        out_specs=pl.BlockSpec((TOKEN_TILE, 3 * D_MODEL), lambda i: (i, 0)),
        out_shape=jax.ShapeDtypeStruct((n, 3 * D_MODEL), bf16),
        compiler_params=pltpu.CompilerParams(
            dimension_semantics=("parallel",), vmem_limit_bytes=VMEM_LIMIT),
    )(x, gain, w, q_gain, k_gain)


def _neg_abs(x):
    sign = jnp.uint32(0x80000000)
    return pltpu.bitcast(pltpu.bitcast(x, jnp.uint32) | sign, f32)


def _split_trunc(x):
    hi = pltpu.bitcast(pltpu.bitcast(x, jnp.uint32) & jnp.uint32(0xFFFF0000), f32)
    return hi.astype(bf16), (x - hi).astype(bf16)


def _sb_attn_kernel(q_ref, k_ref, v_ref, o_ref, kcat_ref, vcat_ref, lb_ref, cum_ref, rest_ref,
                    acc_ref):
    t = SB_TILE
    g = pl.program_id(2)
    heads = range(HEADS_PER_TILE)

    @pl.when(g == 0)
    def _():
        _split_heads_per_tile(k_ref, v_ref, kcat_ref, vcat_ref, t)

    row = lax.broadcasted_iota(jnp.int32, (t, 2 * t), 0)
    col = lax.broadcasted_iota(jnp.int32, (t, 2 * t), 1)
    causal = col % t < row
    rk = lax.broadcasted_iota(jnp.int32, (2 * t, 2 * t), 0) % t
    ck = lax.broadcasted_iota(jnp.int32, (2 * t, 2 * t), 1)
    cum_mat = jnp.where((rk > ck) | (ck >= t), 1.0, 0.0).astype(bf16)

    queries = [q_ref[0, sub * t:(sub + 1) * t, :] * ATTN_SCALE for sub in range(SB_STEP_TILES)]

    def raw_scores(sub, j):
        return _dot_nt(queries[sub], kcat_ref[j])

    def scores(z, diagonal):
        decay = jnp.maximum(z, 0.0) + jnp.log(1.0 + jnp.exp(_neg_abs(z)))
        if diagonal:
            decay = jnp.where(causal, decay, 0.0)
        cums = []
        for head in heads:
            hi, lo = _split_trunc(decay[:, head * t:(head + 1) * t])
            cums.append(_dot(jnp.concatenate([hi, lo], axis=1), cum_mat))
        return z - decay, cums

    def weights(log_beta, cums, rests, j, diagonal, exists=None):
        logs = [log_beta[:, head * t:(head + 1) * t] - cums[head][:, :t] for head in heads]
        if rests is None:
            rests = [cums[head][:, t:] for head in heads]
        else:
            logs = [logs[head] - rests[head] for head in heads]
            rests = [rests[head] + cums[head][:, t:] for head in heads]
        a = jnp.exp(jnp.concatenate(logs, axis=1))
        if diagonal:
            a = jnp.where(causal, a, 0.0)
        if exists is not None:
            a = jnp.where(exists, a, 0.0)
        return rests, _dot(a.astype(bf16), vcat_ref[j])

    def first_sweep(tile_of, depth_of):
        slots = [(sub, n) for sub in range(SB_STEP_TILES) for n in range(depth_of(sub))]
        zz = {}
        rests, acc = None, None

        def stage_scores(u):
            sub, n = slots[u]
            zz[u] = raw_scores(sub, tile_of(sub) - n)

        def stage_cumsum(u):
            log_beta, cums = scores(zz.pop(u), slots[u][1] == 0)
            lb_ref[u] = log_beta
            for head in heads:
                cum_ref[u, head] = cums[head]

        def stage_output(u):
            nonlocal rests, acc
            sub, n = slots[u]
            rests, out = weights(lb_ref[u], [cum_ref[u, head] for head in heads],
                                 None if n == 0 else rests, tile_of(sub) - n, n == 0)
            acc = out if n == 0 else acc + out
            if n == depth_of(sub) - 1:
                for head in heads:
                    rest_ref[sub, head] = rests[head]
                acc_ref[sub] = acc

        for step in range(len(slots) + 2):
            if step < len(slots):
                stage_scores(step)
            if 1 <= step <= len(slots):
                stage_cumsum(step - 1)
            if step >= 2:
                stage_output(step - 2)

    @pl.when(g == 0)
    def _():
        first_sweep(lambda sub: sub, lambda sub: min(SB_FAST_TILES, sub + 1))

    @pl.when(g > 0)
    def _():
        first_sweep(lambda sub: g * SB_STEP_TILES + sub, lambda sub: SB_FAST_TILES)

    def least_decay():
        return jnp.min(functools.reduce(
            jnp.minimum, [rest_ref[sub, head] for sub in range(SB_STEP_TILES) for head in heads]))

    last_tile = g * SB_STEP_TILES + SB_STEP_TILES - 1

    def cond(state):
        n, least = state
        return jnp.logical_and(n <= last_tile, least < SB_DECAY_CUTOFF)

    def body(state):
        n, _ = state
        for sub in range(SB_STEP_TILES):
            j = g * SB_STEP_TILES + sub - n
            log_beta, cums = scores(raw_scores(sub, jnp.maximum(j, 0)), False)
            rests, out = weights(log_beta, cums, [rest_ref[sub, head] for head in heads],
                                 jnp.maximum(j, 0), False, exists=j >= 0)
            for head in heads:
                rest_ref[sub, head] = rests[head]
            acc_ref[sub] = acc_ref[sub] + out
        return n + 1, least_decay()

    lax.while_loop(cond, body, (jnp.int32(SB_FAST_TILES), least_decay()))

    for sub in range(SB_STEP_TILES):
        o_ref[0, sub * t:(sub + 1) * t, :] = acc_ref[sub].astype(o_ref.dtype)


def _sb_attention(qkv):
    b, s, _ = qkv.shape
    t = SB_TILE
    step = SB_STEP_TILES * t
    slots = SB_STEP_TILES * SB_FAST_TILES
    return pl.pallas_call(
        _sb_attn_kernel,
        name="sb_attention",
        grid=(b, N_PAIRS, s // step),
        in_specs=[
            pl.BlockSpec((1, step, LANES), lambda bi, p, i: (bi, i, p)),
            pl.BlockSpec((1, s, LANES), lambda bi, p, i: (bi, 0, N_PAIRS + p)),
            pl.BlockSpec((1, s, LANES), lambda bi, p, i: (bi, 0, 2 * N_PAIRS + p)),
        ],
        out_specs=pl.BlockSpec((1, step, LANES), lambda bi, p, i: (bi, i, p)),
        out_shape=jax.ShapeDtypeStruct((b, s, D_MODEL), bf16),
        scratch_shapes=[pltpu.VMEM((s // t, HEADS_PER_TILE * t, LANES), bf16),
                        pltpu.VMEM((s // t, HEADS_PER_TILE * t, LANES), bf16),
                        pltpu.VMEM((slots, t, HEADS_PER_TILE * t), f32),
                        pltpu.VMEM((slots, HEADS_PER_TILE, t, 2 * t), f32),
                        pltpu.VMEM((SB_STEP_TILES, HEADS_PER_TILE, t, t), f32),
                        pltpu.VMEM((SB_STEP_TILES, t, LANES), f32)],
        compiler_params=pltpu.CompilerParams(
            dimension_semantics=("arbitrary", "arbitrary", "arbitrary"),
            vmem_limit_bytes=VMEM_LIMIT),
    )(qkv, qkv, qkv)


def _chunk_bias_kernel(u_ref, tab_ref):
    t = CA_TILE
    pair = pl.program_id(0)
    r = lax.broadcasted_iota(jnp.int32, (t, t), 0)
    c = lax.broadcasted_iota(jnp.int32, (t, t), 1)
    for head in range(HEADS_PER_TILE):
        u = jnp.broadcast_to(u_ref[pl.ds(pair * HEADS_PER_TILE + head, 1), :], (t, CA_ROLL_WIDTH))
        skew = pltpu.roll(u, 0, 1, stride=1, stride_axis=0)
        for kt in range(CA_KEY_TILES):
            chunk_dist = r // CHUNK + LEFT_CHUNKS - (kt * t + c) // CHUNK
            in_band = (chunk_dist >= 0) & (chunk_dist <= LEFT_CHUNKS)
            tab_ref[0, kt, :, head * t:(head + 1) * t] = jnp.where(
                in_band, skew[:, kt * t:(kt + 1) * t], MASK_VALUE)


def _chunk_bias_table(rel_bias):
    h = rel_bias.shape[0]
    far = rel_bias[:, 2 * MAX_REL:]
    near = rel_bias[:, 2 * MAX_REL - (CA_BAND - MAX_REL) + 1:][:, ::-1]
    u = jnp.concatenate([jnp.broadcast_to(far, (h, CA_BAND - 2 * MAX_REL + CA_TILE)), near,
                         jnp.broadcast_to(far, (h, CA_ROLL_WIDTH - CA_BAND))], axis=1)
    tile = (CA_KEY_TILES, CA_TILE, HEADS_PER_TILE * CA_TILE)
    return pl.pallas_call(
        _chunk_bias_kernel,
        name="chunk_bias_table",
        grid=(h // HEADS_PER_TILE,),
        in_specs=[pl.BlockSpec((h, CA_ROLL_WIDTH), lambda i: (0, 0))],
        out_specs=pl.BlockSpec((1,) + tile, lambda i: (i, 0, 0, 0)),
        out_shape=jax.ShapeDtypeStruct((h // HEADS_PER_TILE,) + tile, f32),
    )(u.astype(f32))


def _chunk_attn_kernel(q_ref, k_ref, v_ref, bias_ref, o_ref, kcat_ref, vcat_ref):
    t = CA_TILE
    g = pl.program_id(2)
    heads = range(HEADS_PER_TILE)

    @pl.when(g == 0)
    def _():
        _split_heads_per_tile(k_ref, v_ref, kcat_ref, vcat_ref, t)

    lane = lax.broadcasted_iota(jnp.int32, (t, LANES), 1)

    def sweep(tile_of):
        logits = {}

        def stage_logits(sub):
            q = q_ref[0, sub * t:(sub + 1) * t, :] * ATTN_SCALE
            tiles = []
            for kt in range(CA_KEY_TILES):
                j = tile_of(sub) - (CA_KEY_TILES - 1) + kt
                if isinstance(j, int) and j < 0:
                    continue
                tiles.append((j, _dot_nt(q, kcat_ref[j]) + bias_ref[0, kt]))
            logits[sub] = tiles

        def stage_output(sub):
            tiles = logits.pop(sub)
            top = functools.reduce(jnp.maximum, [s for _, s in tiles])
            tops = [jnp.max(top[:, head * t:(head + 1) * t], axis=-1, keepdims=True)
                    for head in heads]
            acc, denom = None, None
            for j, s in tiles:
                p = jnp.concatenate([jnp.exp(s[:, head * t:(head + 1) * t] - tops[head])
                                     for head in heads], axis=1)
                out = _dot(p.astype(bf16), vcat_ref[j])
                denom = p if denom is None else denom + p
                acc = out if acc is None else acc + out
            sums = [jnp.sum(denom[:, head * t:(head + 1) * t], axis=-1, keepdims=True)
                    for head in heads]
            out = acc / jnp.where(lane < HEAD_DIM, sums[0], sums[1])
            o_ref[0, sub * t:(sub + 1) * t, :] = out.astype(o_ref.dtype)

        for step in range(CA_STEP_TILES + 1):
            if step < CA_STEP_TILES:
                stage_logits(step)
            if step >= 1:
                stage_output(step - 1)

    @pl.when(g == 0)
    def _():
        sweep(lambda sub: sub)

    @pl.when(g > 0)
    def _():
        sweep(lambda sub: g * CA_STEP_TILES + sub)


def _chunk_attention(qkv, rel_bias):
    b, s, _ = qkv.shape
    t = CA_TILE
    step = CA_STEP_TILES * t
    table = _chunk_bias_table(rel_bias)
    return pl.pallas_call(
        _chunk_attn_kernel,
        name="chunk_attention",
        grid=(b, N_PAIRS, s // step),
        in_specs=[
            pl.BlockSpec((1, step, LANES), lambda bi, p, i: (bi, i, p)),
            pl.BlockSpec((1, s, LANES), lambda bi, p, i: (bi, 0, N_PAIRS + p)),
            pl.BlockSpec((1, s, LANES), lambda bi, p, i: (bi, 0, 2 * N_PAIRS + p)),
            pl.BlockSpec((1, CA_KEY_TILES, t, HEADS_PER_TILE * t), lambda bi, p, i: (p, 0, 0, 0)),
        ],
        out_specs=pl.BlockSpec((1, step, LANES), lambda bi, p, i: (bi, i, p)),
        out_shape=jax.ShapeDtypeStruct((b, s, D_MODEL), bf16),
        scratch_shapes=[pltpu.VMEM((s // t, HEADS_PER_TILE * t, LANES), bf16),
                        pltpu.VMEM((s // t, HEADS_PER_TILE * t, LANES), bf16)],
        compiler_params=pltpu.CompilerParams(
            dimension_semantics=("arbitrary", "arbitrary", "arbitrary"),
            vmem_limit_bytes=VMEM_LIMIT),
    )(qkv, qkv, qkv, table)


def _proj_mlp_kernel(x_ref, o_ref, wo_ref, g_ref, wup_ref, wdn_ref, out_ref):
    x = x_ref[...] + _dot(o_ref[...], wo_ref[...])
    h = (x * _rms_scale(x) * g_ref[...]).astype(bf16)
    acc = x
    for c in range(D_FF // FF_TILE):
        cols = slice(c * FF_TILE, (c + 1) * FF_TILE)
        u = jnp.square(jnp.maximum(_dot(h, wup_ref[:, cols]), 0.0))
        acc = acc + _dot(u.astype(bf16), wdn_ref[cols, :])
    out_ref[...] = acc


def _proj_mlp(x, o, wo, gain, wup, wdn):
    n = x.shape[0]
    const = lambda i: (0, 0)
    rows = lambda i: (i, 0)
    return pl.pallas_call(
        _proj_mlp_kernel,
        name="proj_mlp",
        grid=(n // TOKEN_TILE,),
        in_specs=[
            pl.BlockSpec((TOKEN_TILE, D_MODEL), rows),
            pl.BlockSpec((TOKEN_TILE, D_MODEL), rows),
            pl.BlockSpec((D_MODEL, D_MODEL), const, pipeline_mode=pl.Buffered(1)),
            pl.BlockSpec((1, D_MODEL), const),
            pl.BlockSpec((D_MODEL, D_FF), const, pipeline_mode=pl.Buffered(1)),
            pl.BlockSpec((D_FF, D_MODEL), const, pipeline_mode=pl.Buffered(1)),
        ],
        out_specs=pl.BlockSpec((TOKEN_TILE, D_MODEL), rows),
        out_shape=jax.ShapeDtypeStruct((n, D_MODEL), f32),
        compiler_params=pltpu.CompilerParams(
            dimension_semantics=("parallel",), vmem_limit_bytes=VMEM_LIMIT),
    )(x, o, wo, gain, wup, wdn)


def kernel(x, mix_norm, w_qkv, w_o, q_norm, k_norm, rel_bias, ffn_norm, w_up, w_down):
    b, s, d = x.shape
    depth = w_qkv.shape[0]
    x = x.reshape(b * s, d)
    unit_gain = jnp.ones((1, LANES), f32)
    for layer in range(depth):
        chunked = layer % 2 == 1
        idx = layer // 2
        if chunked:
            q_gain = jnp.tile(q_norm[idx], HEADS_PER_TILE)[None]
            k_gain = jnp.tile(k_norm[idx], HEADS_PER_TILE)[None]
        else:
            q_gain = k_gain = unit_gain
        qkv = _norm_qkv(x, mix_norm[layer][None], w_qkv[layer].astype(bf16),
                        q_gain, k_gain, chunked)
        qkv = qkv.reshape(b, s, 3 * d)
        o = _chunk_attention(qkv, rel_bias[idx]) if chunked else _sb_attention(qkv)
        x = _proj_mlp(x, o.reshape(b * s, d), w_o[layer].astype(bf16), ffn_norm[layer][None],
                      w_up[layer].astype(bf16), w_down[layer].astype(bf16))
    return x.reshape(b, s, d)
```

```python
import functools

import jax
import jax.numpy as jnp
from jax import lax
from jax.experimental import pallas as pl
from jax.experimental.pallas import tpu as pltpu

D_MODEL = 1024
N_HEADS = 16
HEAD_DIM = D_MODEL // N_HEADS
D_FF = 4 * D_MODEL
CHUNK = 64
LEFT_CHUNKS = 8
MAX_REL = 256
RMS_EPS = 1e-6

LANES = 128
HEADS_PER_TILE = LANES // HEAD_DIM
N_PAIRS = N_HEADS // HEADS_PER_TILE
ATTN_SCALE = HEAD_DIM ** -0.5

TOKEN_TILE = 512
FF_TILE = 1024
SB_TILE = 128
SB_STEP_TILES = 8
SB_FAST_TILES = 3
CA_TILE = 2 * CHUNK
CA_KEY_TILES = LEFT_CHUNKS * CHUNK // CA_TILE + 1
CA_STEP_TILES = 8
CA_BAND = CA_KEY_TILES * CA_TILE
CA_ROLL_WIDTH = 1024
MASK_VALUE = -1e30

SB_DECAY_CUTOFF = 110.0
SB_DECAY_DONE = 1e30
LOG2_E = 1.4426950408889634

VMEM_LIMIT = 56 * 1024 * 1024

f32 = jnp.float32
bf16 = jnp.bfloat16


def _rms_scale(x):
    return lax.rsqrt(jnp.mean(x * x, axis=-1, keepdims=True) + RMS_EPS)


def _split_bf16(x):
    hi = x.astype(bf16)
    lo = (x - hi.astype(f32)).astype(bf16)
    return hi, lo


def _dot(a, b):
    return jnp.dot(a, b, preferred_element_type=f32)


def _dot_nt(a, b):
    return lax.dot_general(a, b, (((1,), (1,)), ((), ())), preferred_element_type=f32)


def _split_heads_per_tile(k_ref, v_ref, kcat_ref, vcat_ref, t, row_sums=False):
    lane = lax.broadcasted_iota(jnp.int32, (t, LANES), 1)

    def build(j, carry):
        rows = pl.ds(pl.multiple_of(j * t, t), t)
        k = k_ref[0, rows, :]
        v = v_ref[0, rows, :]
        for head in range(HEADS_PER_TILE):
            own = lane // HEAD_DIM == head
            kcat_ref[j, head * t:(head + 1) * t, :] = jnp.where(own, k, jnp.zeros_like(k))
            vcat_ref[j, head * t:(head + 1) * t, :LANES] = jnp.where(own, v, jnp.zeros_like(v))
            if row_sums:
                vcat_ref[j, head * t:(head + 1) * t, LANES:] = jnp.where(own, 1.0, 0.0).astype(bf16)
        return carry

    lax.fori_loop(0, k_ref.shape[1] // t, build, 0)


def _norm_qkv_kernel(x_ref, g_ref, w_ref, qg_ref, kg_ref, out_ref, *, head_norm):
    x = x_ref[...]
    h = (x * _rms_scale(x) * g_ref[...]).astype(bf16)
    if head_norm:
        r = lax.broadcasted_iota(jnp.int32, (LANES, LANES), 0) // HEAD_DIM
        c = lax.broadcasted_iota(jnp.int32, (LANES, LANES), 1) // HEAD_DIM
        head_mean = jnp.where(r == c, 1.0 / HEAD_DIM, 0.0).astype(bf16)
    for part in range(3):
        cols = slice(part * D_MODEL, (part + 1) * D_MODEL)
        acc = _dot(h, w_ref[:, cols])
        if head_norm and part < 2:
            gain = (qg_ref if part == 0 else kg_ref)[...]
            for p in range(N_PAIRS):
                blk = acc[:, p * LANES:(p + 1) * LANES]
                hi, lo = _split_bf16(blk * blk)
                ms = _dot(hi, head_mean) + _dot(lo, head_mean)
                y = blk * lax.rsqrt(ms + RMS_EPS) * gain
                out_ref[:, part * D_MODEL + p * LANES:part * D_MODEL + (p + 1) * LANES] = y.astype(bf16)
        else:
            out_ref[:, cols] = acc.astype(bf16)


def _norm_qkv(x, gain, w, q_gain, k_gain, head_norm):
    n = x.shape[0]
    const = lambda i: (0, 0)
    return pl.pallas_call(
        functools.partial(_norm_qkv_kernel, head_norm=head_norm),
        name="norm_qkv_hn" if head_norm else "norm_qkv",
        grid=(n // TOKEN_TILE,),
        in_specs=[
            pl.BlockSpec((TOKEN_TILE, D_MODEL), lambda i: (i, 0)),
            pl.BlockSpec((1, D_MODEL), const),
            pl.BlockSpec((D_MODEL, 3 * D_MODEL), const, pipeline_mode=pl.Buffered(1)),
            pl.BlockSpec((1, LANES), const),
            pl.BlockSpec((1, LANES), const),
        ],
        out_specs=pl.BlockSpec((TOKEN_TILE, 3 * D_MODEL), lambda i: (i, 0)),
        out_shape=jax.ShapeDtypeStruct((n, 3 * D_MODEL), bf16),
        compiler_params=pltpu.CompilerParams(
            dimension_semantics=("parallel",), vmem_limit_bytes=VMEM_LIMIT),
    )(x, gain, w, q_gain, k_gain)


def _split_trunc(x):
    hi = pltpu.bitcast(pltpu.bitcast(x, jnp.uint32) & jnp.uint32(0xFFFF0000), f32)
    return hi.astype(bf16), (x - hi).astype(bf16)


def _sb_attn_kernel(q_ref, k_ref, v_ref, o_ref, kcat_ref, vcat_ref, lb_ref, cum_ref, rest_ref,
                    acc_ref):
    t = SB_TILE
    g = pl.program_id(2)
    heads = range(HEADS_PER_TILE)

    @pl.when(g == 0)
    def _():
        _split_heads_per_tile(k_ref, v_ref, kcat_ref, vcat_ref, t)

    row = lax.broadcasted_iota(jnp.int32, (t, 2 * t), 0)
    col = lax.broadcasted_iota(jnp.int32, (t, 2 * t), 1)
    causal = col % t < row
    rk = lax.broadcasted_iota(jnp.int32, (2 * t, 2 * t), 0) % t
    ck = lax.broadcasted_iota(jnp.int32, (2 * t, 2 * t), 1)
    cum_mat = jnp.where((rk > ck) | (ck >= t), 1.0, 0.0).astype(bf16)

    queries = [q_ref[0, sub * t:(sub + 1) * t, :] * ATTN_SCALE for sub in range(SB_STEP_TILES)]

    def raw_scores(sub, j):
        return _dot_nt(queries[sub], kcat_ref[j])

    def scores(z, diagonal):
        decay = jnp.maximum(z, 0.0) + jnp.log(1.0 + jnp.exp2(jnp.abs(z) * -LOG2_E))
        if diagonal:
            decay = jnp.where(causal, decay, 0.0)
        cums = []
        for head in heads:
            hi, lo = _split_trunc(decay[:, head * t:(head + 1) * t])
            cums.append(_dot(jnp.concatenate([hi, lo], axis=1), cum_mat))
        return z - decay, cums

    def weights(log_beta, cums, rests, j, diagonal):
        logs = [log_beta[:, head * t:(head + 1) * t] - cums[head][:, :t] for head in heads]
        if rests is None:
            rests = [cums[head][:, t:] for head in heads]
        else:
            logs = [logs[head] - rests[head] for head in heads]
            rests = [rests[head] + cums[head][:, t:] for head in heads]
        a = jnp.exp(jnp.concatenate(logs, axis=1))
        if diagonal:
            a = jnp.where(causal, a, 0.0)
        return rests, _dot(a.astype(bf16), vcat_ref[j])

    def first_sweep(tile_of, depth_of):
        slots = [(sub, n) for sub in range(SB_STEP_TILES) for n in range(depth_of(sub))]
        zz = {}
        rests, acc = None, None

        def stage_scores(u):
            sub, n = slots[u]
            zz[u] = raw_scores(sub, tile_of(sub) - n)

        def stage_cumsum(u):
            log_beta, cums = scores(zz.pop(u), slots[u][1] == 0)
            lb_ref[u] = log_beta
            for head in heads:
                cum_ref[u, head] = cums[head]

        def stage_output(u):
            nonlocal rests, acc
            sub, n = slots[u]
            rests, out = weights(lb_ref[u], [cum_ref[u, head] for head in heads],
                                 None if n == 0 else rests, tile_of(sub) - n, n == 0)
            acc = out if n == 0 else acc + out
            if n == depth_of(sub) - 1:
                for head in heads:
                    rest_ref[sub, head] = rests[head]
                acc_ref[sub] = acc

        for step in range(len(slots) + 2):
            if step < len(slots):
                stage_scores(step)
            if 1 <= step <= len(slots):
                stage_cumsum(step - 1)
            if step >= 2:
                stage_output(step - 2)

    @pl.when(g == 0)
    def _():
        first_sweep(lambda sub: sub, lambda sub: min(SB_FAST_TILES, sub + 1))

    @pl.when(g > 0)
    def _():
        first_sweep(lambda sub: g * SB_STEP_TILES + sub, lambda sub: SB_FAST_TILES)

    def least_decay(rests):
        return jnp.min(functools.reduce(jnp.minimum, rests))

    def tile_rests(sub):
        return [rest_ref[sub, head] for head in heads]

    @pl.when(least_decay([r for sub in range(SB_STEP_TILES) for r in tile_rests(sub)])
             < SB_DECAY_CUTOFF)
    def _():
        def cond(state):
            return functools.reduce(jnp.minimum, state[1:]) < SB_DECAY_CUTOFF

        def body(state):
            n, leasts = state[0], state[1:]
            updated = []
            for sub in range(SB_STEP_TILES):
                j = g * SB_STEP_TILES + sub - n

                def visit(sub=sub, j=j):
                    j = jnp.maximum(j, 0)
                    log_beta, cums = scores(raw_scores(sub, j), False)
                    rests, out = weights(log_beta, cums, tile_rests(sub), j, False)
                    for head in heads:
                        rest_ref[sub, head] = rests[head]
                    acc_ref[sub] = acc_ref[sub] + out
                    return least_decay(rests)

                def skip(j=j, least=leasts[sub]):
                    return jnp.where(j >= 0, least, SB_DECAY_DONE)

                pending = jnp.logical_and(j >= 0, leasts[sub] < SB_DECAY_CUTOFF)
                updated.append(lax.cond(pending, visit, skip))
            return (n + 1, *updated)

        lax.while_loop(cond, body, (jnp.int32(SB_FAST_TILES),
                                    *[least_decay(tile_rests(sub)) for sub in range(SB_STEP_TILES)]))

    for sub in range(SB_STEP_TILES):
        o_ref[0, sub * t:(sub + 1) * t, :] = acc_ref[sub].astype(o_ref.dtype)


def _sb_attention(qkv):
    b, s, _ = qkv.shape
    t = SB_TILE
    step = SB_STEP_TILES * t
    slots = SB_STEP_TILES * SB_FAST_TILES
    return pl.pallas_call(
        _sb_attn_kernel,
        name="sb_attention",
        grid=(b, N_PAIRS, s // step),
        in_specs=[
            pl.BlockSpec((1, step, LANES), lambda bi, p, i: (bi, i, p)),
            pl.BlockSpec((1, s, LANES), lambda bi, p, i: (bi, 0, N_PAIRS + p)),
            pl.BlockSpec((1, s, LANES), lambda bi, p, i: (bi, 0, 2 * N_PAIRS + p)),
        ],
        out_specs=pl.BlockSpec((1, step, LANES), lambda bi, p, i: (bi, i, p)),
        out_shape=jax.ShapeDtypeStruct((b, s, D_MODEL), bf16),
        scratch_shapes=[pltpu.VMEM((s // t, HEADS_PER_TILE * t, LANES), bf16),
                        pltpu.VMEM((s // t, HEADS_PER_TILE * t, LANES), bf16),
                        pltpu.VMEM((slots, t, HEADS_PER_TILE * t), f32),
                        pltpu.VMEM((slots, HEADS_PER_TILE, t, 2 * t), f32),
                        pltpu.VMEM((SB_STEP_TILES, HEADS_PER_TILE, t, t), f32),
                        pltpu.VMEM((SB_STEP_TILES, t, LANES), f32)],
        compiler_params=pltpu.CompilerParams(
            dimension_semantics=("arbitrary", "arbitrary", "arbitrary"),
            vmem_limit_bytes=VMEM_LIMIT),
    )(qkv, qkv, qkv)


def _chunk_bias_kernel(u_ref, tab_ref):
    t = CA_TILE
    pair = pl.program_id(0)
    r = lax.broadcasted_iota(jnp.int32, (t, t), 0)
    c = lax.broadcasted_iota(jnp.int32, (t, t), 1)
    for head in range(HEADS_PER_TILE):
        u = jnp.broadcast_to(u_ref[pl.ds(pair * HEADS_PER_TILE + head, 1), :], (t, CA_ROLL_WIDTH))
        skew = pltpu.roll(u, 0, 1, stride=1, stride_axis=0)
        for kt in range(CA_KEY_TILES):
            chunk_dist = r // CHUNK + LEFT_CHUNKS - (kt * t + c) // CHUNK
            in_band = (chunk_dist >= 0) & (chunk_dist <= LEFT_CHUNKS)
            tab_ref[0, kt, :, head * t:(head + 1) * t] = jnp.where(
                in_band, skew[:, kt * t:(kt + 1) * t], MASK_VALUE)


def _chunk_bias_table(rel_bias):
    h = rel_bias.shape[0]
    far = rel_bias[:, 2 * MAX_REL:]
    near = rel_bias[:, 2 * MAX_REL - (CA_BAND - MAX_REL) + 1:][:, ::-1]
    u = jnp.concatenate([jnp.broadcast_to(far, (h, CA_BAND - 2 * MAX_REL + CA_TILE)), near,
                         jnp.broadcast_to(far, (h, CA_ROLL_WIDTH - CA_BAND))], axis=1)
    tile = (CA_KEY_TILES, CA_TILE, HEADS_PER_TILE * CA_TILE)
    return pl.pallas_call(
        _chunk_bias_kernel,
        name="chunk_bias_table",
        grid=(h // HEADS_PER_TILE,),
        in_specs=[pl.BlockSpec((h, CA_ROLL_WIDTH), lambda i: (0, 0))],
        out_specs=pl.BlockSpec((1,) + tile, lambda i: (i, 0, 0, 0)),
        out_shape=jax.ShapeDtypeStruct((h // HEADS_PER_TILE,) + tile, f32),
    )(u.astype(f32))


def _chunk_attn_kernel(q_ref, k_ref, v_ref, bias_ref, o_ref, kcat_ref, vcat_ref, logit_ref):
    t = CA_TILE
    g = pl.program_id(2)
    heads = range(HEADS_PER_TILE)

    @pl.when(g == 0)
    def _():
        _split_heads_per_tile(k_ref, v_ref, kcat_ref, vcat_ref, t, row_sums=True)

    def sweep(tile_of):
        logits = {}

        def stage_logits(sub):
            q = q_ref[0, sub * t:(sub + 1) * t, :] * ATTN_SCALE
            tiles = []
            for kt in range(CA_KEY_TILES):
                j = tile_of(sub) - (CA_KEY_TILES - 1) + kt
                if isinstance(j, int) and j < 0:
                    continue
                logit_ref[sub % 2, kt] = _dot_nt(q, kcat_ref[j]) + bias_ref[0, kt]
                tiles.append((j, kt))
            logits[sub] = tiles

        def stage_output(sub):
            tiles = [(j, logit_ref[sub % 2, kt]) for j, kt in logits.pop(sub)]
            top = functools.reduce(jnp.maximum, [s for _, s in tiles])
            tops = [jnp.max(top[:, head * t:(head + 1) * t], axis=-1, keepdims=True)
                    for head in heads]
            acc = None
            for j, s in tiles:
                p = jnp.concatenate([jnp.exp(s[:, head * t:(head + 1) * t] - tops[head])
                                     for head in heads], axis=1)
                out = _dot(p.astype(bf16), vcat_ref[j])
                acc = out if acc is None else acc + out
            o_ref[0, sub * t:(sub + 1) * t, :] = (acc[:, :LANES] / acc[:, LANES:]).astype(o_ref.dtype)

        for step in range(CA_STEP_TILES + 1):
            if step < CA_STEP_TILES:
                stage_logits(step)
            if step >= 1:
                stage_output(step - 1)

    @pl.when(g == 0)
    def _():
        sweep(lambda sub: sub)

    @pl.when(g > 0)
    def _():
        sweep(lambda sub: g * CA_STEP_TILES + sub)


def _chunk_attention(qkv, rel_bias):
    b, s, _ = qkv.shape
    t = CA_TILE
    step = CA_STEP_TILES * t
    table = _chunk_bias_table(rel_bias)
    return pl.pallas_call(
        _chunk_attn_kernel,
        name="chunk_attention",
        grid=(b, N_PAIRS, s // step),
        in_specs=[
            pl.BlockSpec((1, step, LANES), lambda bi, p, i: (bi, i, p)),
            pl.BlockSpec((1, s, LANES), lambda bi, p, i: (bi, 0, N_PAIRS + p)),
            pl.BlockSpec((1, s, LANES), lambda bi, p, i: (bi, 0, 2 * N_PAIRS + p)),
            pl.BlockSpec((1, CA_KEY_TILES, t, HEADS_PER_TILE * t), lambda bi, p, i: (p, 0, 0, 0)),
        ],
        out_specs=pl.BlockSpec((1, step, LANES), lambda bi, p, i: (bi, i, p)),
        out_shape=jax.ShapeDtypeStruct((b, s, D_MODEL), bf16),
        scratch_shapes=[pltpu.VMEM((s // t, HEADS_PER_TILE * t, LANES), bf16),
                        pltpu.VMEM((s // t, HEADS_PER_TILE * t, 2 * LANES), bf16),
                        pltpu.VMEM((2, CA_KEY_TILES, t, HEADS_PER_TILE * t), f32)],
        compiler_params=pltpu.CompilerParams(
            dimension_semantics=("arbitrary", "arbitrary", "arbitrary"),
            vmem_limit_bytes=VMEM_LIMIT),
    )(qkv, qkv, qkv, table)


def _proj_mlp_kernel(x_ref, o_ref, wo_ref, g_ref, wup_ref, wdn_ref, out_ref):
    x = x_ref[...] + _dot(o_ref[...], wo_ref[...])
    h = (x * _rms_scale(x) * g_ref[...]).astype(bf16)
    acc = x
    for c in range(D_FF // FF_TILE):
        cols = slice(c * FF_TILE, (c + 1) * FF_TILE)
        u = jnp.square(jnp.maximum(_dot(h, wup_ref[:, cols]), 0.0))
        acc = acc + _dot(u.astype(bf16), wdn_ref[cols, :])
    out_ref[...] = acc


def _proj_mlp(x, o, wo, gain, wup, wdn):
    n = x.shape[0]
    const = lambda i: (0, 0)
    rows = lambda i: (i, 0)
    return pl.pallas_call(
        _proj_mlp_kernel,
        name="proj_mlp",
        grid=(n // TOKEN_TILE,),
        in_specs=[
            pl.BlockSpec((TOKEN_TILE, D_MODEL), rows),
            pl.BlockSpec((TOKEN_TILE, D_MODEL), rows),
            pl.BlockSpec((D_MODEL, D_MODEL), const, pipeline_mode=pl.Buffered(1)),
            pl.BlockSpec((1, D_MODEL), const),
            pl.BlockSpec((D_MODEL, D_FF), const, pipeline_mode=pl.Buffered(1)),
            pl.BlockSpec((D_FF, D_MODEL), const, pipeline_mode=pl.Buffered(1)),
        ],
        out_specs=pl.BlockSpec((TOKEN_TILE, D_MODEL), rows),
        out_shape=jax.ShapeDtypeStruct((n, D_MODEL), f32),
        compiler_params=pltpu.CompilerParams(
            dimension_semantics=("parallel",), vmem_limit_bytes=VMEM_LIMIT),
    )(x, o, wo, gain, wup, wdn)


def kernel(x, mix_norm, w_qkv, w_o, q_norm, k_norm, rel_bias, ffn_norm, w_up, w_down):
    b, s, d = x.shape
    depth = w_qkv.shape[0]
    x = x.reshape(b * s, d)
    unit_gain = jnp.ones((1, LANES), f32)
    for layer in range(depth):
        chunked = layer % 2 == 1
        idx = layer // 2
        if chunked:
            q_gain = jnp.tile(q_norm[idx], HEADS_PER_TILE)[None]
            k_gain = jnp.tile(k_norm[idx], HEADS_PER_TILE)[None]
        else:
            q_gain = k_gain = unit_gain
        qkv = _norm_qkv(x, mix_norm[layer][None], w_qkv[layer].astype(bf16),
                        q_gain, k_gain, chunked)
        qkv = qkv.reshape(b, s, 3 * d)
        o = _chunk_attention(qkv, rel_bias[idx]) if chunked else _sb_attention(qkv)
        x = _proj_mlp(x, o.reshape(b * s, d), w_o[layer].astype(bf16), ffn_norm[layer][None],
                      w_up[layer].astype(bf16), w_down[layer].astype(bf16))
    return x.reshape(b, s, d)
```

```python
import functools

import jax
import jax.numpy as jnp
from jax import lax
from jax.experimental import pallas as pl
from jax.experimental.pallas import tpu as pltpu

D_MODEL = 1024
N_HEADS = 16
HEAD_DIM = D_MODEL // N_HEADS
D_FF = 4 * D_MODEL
CHUNK = 64
LEFT_CHUNKS = 8
MAX_REL = 256
RMS_EPS = 1e-6

LANES = 128
HEADS_PER_TILE = LANES // HEAD_DIM
N_PAIRS = N_HEADS // HEADS_PER_TILE
ATTN_SCALE = HEAD_DIM ** -0.5

TOKEN_TILE = 512
FF_TILE = 1024
KEY_TILE = 128
SB_TILE = KEY_TILE
SB_STEP_TILES = 8
SB_FAST_TILES = 3
CA_TILE = KEY_TILE
CA_KEY_TILES = LEFT_CHUNKS * CHUNK // CA_TILE + 1
CA_STEP_TILES = 8
CA_BAND = CA_KEY_TILES * CA_TILE
CA_ROLL_WIDTH = 1024
MASK_VALUE = -1e30

SB_DECAY_CUTOFF = 110.0
SB_DECAY_DONE = 1e30
LOG2_E = 1.4426950408889634

VMEM_LIMIT = 56 * 1024 * 1024

f32 = jnp.float32
bf16 = jnp.bfloat16


def _rms_scale(x):
    return lax.rsqrt(jnp.mean(x * x, axis=-1, keepdims=True) + RMS_EPS)


def _split_bf16(x):
    hi = x.astype(bf16)
    lo = (x - hi.astype(f32)).astype(bf16)
    return hi, lo


def _dot(a, b):
    return jnp.dot(a, b, preferred_element_type=f32)


def _dot_nt(a, b):
    return lax.dot_general(a, b, (((1,), (1,)), ((), ())), preferred_element_type=f32)


def _norm_qkv_kernel(x_ref, g_ref, w_ref, qg_ref, kg_ref, q_ref, kcat_ref, vcat_ref, *, head_norm):
    t = KEY_TILE
    x = x_ref[...]
    h = (x * _rms_scale(x) * g_ref[...]).astype(bf16)
    lane = lax.broadcasted_iota(jnp.int32, (t, LANES), 1)
    own = [lane // HEAD_DIM == head for head in range(HEADS_PER_TILE)]
    if head_norm:
        r = lax.broadcasted_iota(jnp.int32, (2 * LANES, LANES), 0) % LANES // HEAD_DIM
        c = lax.broadcasted_iota(jnp.int32, (2 * LANES, LANES), 1) // HEAD_DIM
        head_mean = jnp.where(r == c, 1.0 / HEAD_DIM, 0.0).astype(bf16)
    for part, cat_ref in enumerate((None, kcat_ref, vcat_ref)):
        acc = _dot(h, w_ref[:, part * D_MODEL:(part + 1) * D_MODEL])
        for p in range(N_PAIRS):
            blk = acc[:, p * LANES:(p + 1) * LANES]
            if head_norm and part < 2:
                hi, lo = _split_bf16(blk * blk)
                ms = _dot(jnp.concatenate([hi, lo], axis=1), head_mean)
                blk = blk * lax.rsqrt(ms + RMS_EPS) * (qg_ref if part == 0 else kg_ref)[...]
            blk = blk.astype(bf16)
            if cat_ref is None:
                q_ref[:, p * LANES:(p + 1) * LANES] = blk
                continue
            for tile in range(TOKEN_TILE // t):
                rows = blk[tile * t:(tile + 1) * t]
                for head in range(HEADS_PER_TILE):
                    cat_ref[tile, p, head * t:(head + 1) * t, :] = jnp.where(
                        own[head], rows, jnp.zeros_like(rows))


def _norm_qkv(x, gain, w, q_gain, k_gain, head_norm):
    n = x.shape[0]
    const = lambda i: (0, 0)
    cat_block = (TOKEN_TILE // KEY_TILE, N_PAIRS, HEADS_PER_TILE * KEY_TILE, LANES)
    cat_shape = jax.ShapeDtypeStruct((n // KEY_TILE,) + cat_block[1:], bf16)
    cat_spec = pl.BlockSpec(cat_block, lambda i: (i, 0, 0, 0))
    return pl.pallas_call(
        functools.partial(_norm_qkv_kernel, head_norm=head_norm),
        name="norm_qkv_hn" if head_norm else "norm_qkv",
        grid=(n // TOKEN_TILE,),
        in_specs=[
            pl.BlockSpec((TOKEN_TILE, D_MODEL), lambda i: (i, 0)),
            pl.BlockSpec((1, D_MODEL), const),
            pl.BlockSpec((D_MODEL, 3 * D_MODEL), const, pipeline_mode=pl.Buffered(1)),
            pl.BlockSpec((1, LANES), const),
            pl.BlockSpec((1, LANES), const),
        ],
        out_specs=[pl.BlockSpec((TOKEN_TILE, D_MODEL), lambda i: (i, 0)), cat_spec, cat_spec],
        out_shape=[jax.ShapeDtypeStruct((n, D_MODEL), bf16), cat_shape, cat_shape],
        compiler_params=pltpu.CompilerParams(
            dimension_semantics=("parallel",), vmem_limit_bytes=VMEM_LIMIT),
    )(x, gain, w, q_gain, k_gain)


def _split_trunc(x):
    hi = pltpu.bitcast(pltpu.bitcast(x, jnp.uint32) & jnp.uint32(0xFFFF0000), f32)
    return hi.astype(bf16), (x - hi).astype(bf16)


def _sb_attn_kernel(q_ref, kcat_ref, vcat_ref, o_ref, lb_ref, cum_ref, rest_ref, acc_ref):
    t = SB_TILE
    g = pl.program_id(2)
    heads = range(HEADS_PER_TILE)

    row = lax.broadcasted_iota(jnp.int32, (t, 2 * t), 0)
    col = lax.broadcasted_iota(jnp.int32, (t, 2 * t), 1)
    causal = col % t < row
    rk = lax.broadcasted_iota(jnp.int32, (2 * t, 2 * t), 0) % t
    ck = lax.broadcasted_iota(jnp.int32, (2 * t, 2 * t), 1)
    cum_mat = jnp.where((rk > ck) | (ck >= t), 1.0, 0.0).astype(bf16)

    queries = [q_ref[0, sub * t:(sub + 1) * t, :] * ATTN_SCALE for sub in range(SB_STEP_TILES)]

    def raw_scores(sub, j):
        return _dot_nt(queries[sub], kcat_ref[j])

    def scores(z, diagonal):
        decay = jnp.maximum(z, 0.0) + jnp.log(1.0 + jnp.exp2(jnp.abs(z) * -LOG2_E))
        if diagonal:
            decay = jnp.where(causal, decay, 0.0)
        cums = []
        for head in heads:
            hi, lo = _split_trunc(decay[:, head * t:(head + 1) * t])
            cums.append(_dot(jnp.concatenate([hi, lo], axis=1), cum_mat))
        return z - decay, cums

    def weights(log_beta, cums, rests, j, diagonal):
        logs = [log_beta[:, head * t:(head + 1) * t] - cums[head][:, :t] for head in heads]
        if rests is None:
            rests = [cums[head][:, t:] for head in heads]
        else:
            logs = [logs[head] - rests[head] for head in heads]
            rests = [rests[head] + cums[head][:, t:] for head in heads]
        a = jnp.exp(jnp.concatenate(logs, axis=1))
        if diagonal:
            a = jnp.where(causal, a, 0.0)
        return rests, _dot(a.astype(bf16), vcat_ref[j])

    def first_sweep(tile_of, depth_of):
        slots = [(sub, n) for sub in range(SB_STEP_TILES) for n in range(depth_of(sub))]
        zz = {}
        rests, acc = None, None

        def stage_scores(u):
            sub, n = slots[u]
            zz[u] = raw_scores(sub, tile_of(sub) - n)

        def stage_cumsum(u):
            log_beta, cums = scores(zz.pop(u), slots[u][1] == 0)
            lb_ref[u] = log_beta
            for head in heads:
                cum_ref[u, head] = cums[head]

        def stage_output(u):
            nonlocal rests, acc
            sub, n = slots[u]
            rests, out = weights(lb_ref[u], [cum_ref[u, head] for head in heads],
                                 None if n == 0 else rests, tile_of(sub) - n, n == 0)
            acc = out if n == 0 else acc + out
            if n == depth_of(sub) - 1:
                for head in heads:
                    rest_ref[sub, head] = rests[head]
                acc_ref[sub] = acc

        for step in range(len(slots) + 2):
            if step < len(slots):
                stage_scores(step)
            if 1 <= step <= len(slots):
                stage_cumsum(step - 1)
            if step >= 2:
                stage_output(step - 2)

    @pl.when(g == 0)
    def _():
        first_sweep(lambda sub: sub, lambda sub: min(SB_FAST_TILES, sub + 1))

    @pl.when(g > 0)
    def _():
        first_sweep(lambda sub: g * SB_STEP_TILES + sub, lambda sub: SB_FAST_TILES)

    def least_decay(rests):
        return jnp.min(functools.reduce(jnp.minimum, rests))

    def tile_rests(sub):
        return [rest_ref[sub, head] for head in heads]

    @pl.when(least_decay([r for sub in range(SB_STEP_TILES) for r in tile_rests(sub)])
             < SB_DECAY_CUTOFF)
    def _():
        def cond(state):
            return functools.reduce(jnp.minimum, state[1:]) < SB_DECAY_CUTOFF

        def body(state):
            n, leasts = state[0], state[1:]
            updated = []
            for sub in range(SB_STEP_TILES):
                j = g * SB_STEP_TILES + sub - n

                def visit(sub=sub, j=j):
                    j = jnp.maximum(j, 0)
                    log_beta, cums = scores(raw_scores(sub, j), False)
                    rests, out = weights(log_beta, cums, tile_rests(sub), j, False)
                    for head in heads:
                        rest_ref[sub, head] = rests[head]
                    acc_ref[sub] = acc_ref[sub] + out
                    return least_decay(rests)

                def skip(j=j, least=leasts[sub]):
                    return jnp.where(j >= 0, least, SB_DECAY_DONE)

                pending = jnp.logical_and(j >= 0, leasts[sub] < SB_DECAY_CUTOFF)
                updated.append(lax.cond(pending, visit, skip))
            return (n + 1, *updated)

        lax.while_loop(cond, body, (jnp.int32(SB_FAST_TILES),
                                    *[least_decay(tile_rests(sub)) for sub in range(SB_STEP_TILES)]))

    for sub in range(SB_STEP_TILES):
        o_ref[0, sub * t:(sub + 1) * t, :] = acc_ref[sub].astype(o_ref.dtype)


def _attention_specs(b, s, step):
    rows = pl.BlockSpec((1, step, LANES), lambda bi, p, i: (bi, i, p))
    cat = pl.BlockSpec((None, s // KEY_TILE, None, HEADS_PER_TILE * KEY_TILE, LANES),
                       lambda bi, p, i: (bi, 0, p, 0, 0))
    return (b, N_PAIRS, s // step), [rows, cat, cat], rows


def _sb_attention(q, kcat, vcat):
    b, s, _ = q.shape
    t = SB_TILE
    slots = SB_STEP_TILES * SB_FAST_TILES
    grid, in_specs, out_spec = _attention_specs(b, s, SB_STEP_TILES * t)
    return pl.pallas_call(
        _sb_attn_kernel,
        name="sb_attention",
        grid=grid,
        in_specs=in_specs,
        out_specs=out_spec,
        out_shape=jax.ShapeDtypeStruct((b, s, D_MODEL), bf16),
        scratch_shapes=[pltpu.VMEM((slots, t, HEADS_PER_TILE * t), f32),
                        pltpu.VMEM((slots, HEADS_PER_TILE, t, 2 * t), f32),
                        pltpu.VMEM((SB_STEP_TILES, HEADS_PER_TILE, t, t), f32),
                        pltpu.VMEM((SB_STEP_TILES, t, LANES), f32)],
        compiler_params=pltpu.CompilerParams(
            dimension_semantics=("parallel", "parallel", "parallel"),
            vmem_limit_bytes=VMEM_LIMIT),
    )(q, kcat, vcat)


def _chunk_bias_kernel(u_ref, tab_ref):
    t = CA_TILE
    pair = pl.program_id(0)
    r = lax.broadcasted_iota(jnp.int32, (t, t), 0)
    c = lax.broadcasted_iota(jnp.int32, (t, t), 1)
    for head in range(HEADS_PER_TILE):
        u = jnp.broadcast_to(u_ref[pl.ds(pair * HEADS_PER_TILE + head, 1), :], (t, CA_ROLL_WIDTH))
        skew = pltpu.roll(u, 0, 1, stride=1, stride_axis=0)
        for kt in range(CA_KEY_TILES):
            chunk_dist = r // CHUNK + LEFT_CHUNKS - (kt * t + c) // CHUNK
            in_band = (chunk_dist >= 0) & (chunk_dist <= LEFT_CHUNKS)
            tab_ref[0, kt, :, head * t:(head + 1) * t] = jnp.where(
                in_band, skew[:, kt * t:(kt + 1) * t], MASK_VALUE)


def _chunk_bias_table(rel_bias):
    h = rel_bias.shape[0]
    far = rel_bias[:, 2 * MAX_REL:]
    near = rel_bias[:, 2 * MAX_REL - (CA_BAND - MAX_REL) + 1:][:, ::-1]
    u = jnp.concatenate([jnp.broadcast_to(far, (h, CA_BAND - 2 * MAX_REL + CA_TILE)), near,
                         jnp.broadcast_to(far, (h, CA_ROLL_WIDTH - CA_BAND))], axis=1)
    tile = (CA_KEY_TILES, CA_TILE, HEADS_PER_TILE * CA_TILE)
    return pl.pallas_call(
        _chunk_bias_kernel,
        name="chunk_bias_table",
        grid=(h // HEADS_PER_TILE,),
        in_specs=[pl.BlockSpec((h, CA_ROLL_WIDTH), lambda i: (0, 0))],
        out_specs=pl.BlockSpec((1,) + tile, lambda i: (i, 0, 0, 0)),
        out_shape=jax.ShapeDtypeStruct((h // HEADS_PER_TILE,) + tile, f32),
    )(u.astype(f32))


def _chunk_attn_kernel(q_ref, kcat_ref, vcat_ref, bias_ref, o_ref, logit_ref):
    t = CA_TILE
    g = pl.program_id(2)
    heads = range(HEADS_PER_TILE)
    r = lax.broadcasted_iota(jnp.int32, (HEADS_PER_TILE * t, LANES), 0) // t
    c = lax.broadcasted_iota(jnp.int32, (HEADS_PER_TILE * t, LANES), 1) // HEAD_DIM
    head_sums = jnp.where(r == c, 1.0, 0.0).astype(bf16)

    def sweep(tile_of):
        logits = {}

        def stage_logits(sub):
            q = q_ref[0, sub * t:(sub + 1) * t, :] * ATTN_SCALE
            tiles = []
            for kt in range(CA_KEY_TILES):
                j = tile_of(sub) - (CA_KEY_TILES - 1) + kt
                if isinstance(j, int) and j < 0:
                    continue
                logit_ref[sub % 2, kt] = _dot_nt(q, kcat_ref[j]) + bias_ref[0, kt]
                tiles.append((j, kt))
            logits[sub] = tiles

        def stage_output(sub):
            tiles = [(j, logit_ref[sub % 2, kt]) for j, kt in logits.pop(sub)]
            top = functools.reduce(jnp.maximum, [s for _, s in tiles])
            tops = [jnp.max(top[:, head * t:(head + 1) * t], axis=-1, keepdims=True)
                    for head in heads]
            acc = None
            for j, s in tiles:
                p = jnp.concatenate([jnp.exp(s[:, head * t:(head + 1) * t] - tops[head])
                                     for head in heads], axis=1)
                values = jnp.concatenate([vcat_ref[j], head_sums], axis=1)
                out = _dot(p.astype(bf16), values)
                acc = out if acc is None else acc + out
            o_ref[0, sub * t:(sub + 1) * t, :] = (acc[:, :LANES] / acc[:, LANES:]).astype(o_ref.dtype)

        for step in range(CA_STEP_TILES + 1):
            if step < CA_STEP_TILES:
                stage_logits(step)
            if step >= 1:
                stage_output(step - 1)

    @pl.when(g == 0)
    def _():
        sweep(lambda sub: sub)

    @pl.when(g > 0)
    def _():
        sweep(lambda sub: g * CA_STEP_TILES + sub)


def _chunk_attention(q, kcat, vcat, rel_bias):
    b, s, _ = q.shape
    t = CA_TILE
    table = _chunk_bias_table(rel_bias)
    grid, in_specs, out_spec = _attention_specs(b, s, CA_STEP_TILES * t)
    bias_spec = pl.BlockSpec((1, CA_KEY_TILES, t, HEADS_PER_TILE * t),
                             lambda bi, p, i: (p, 0, 0, 0))
    return pl.pallas_call(
        _chunk_attn_kernel,
        name="chunk_attention",
        grid=grid,
        in_specs=in_specs + [bias_spec],
        out_specs=out_spec,
        out_shape=jax.ShapeDtypeStruct((b, s, D_MODEL), bf16),
        scratch_shapes=[pltpu.VMEM((2, CA_KEY_TILES, t, HEADS_PER_TILE * t), f32)],
        compiler_params=pltpu.CompilerParams(
            dimension_semantics=("parallel", "parallel", "parallel"),
            vmem_limit_bytes=VMEM_LIMIT),
    )(q, kcat, vcat, table)


def _proj_mlp_kernel(x_ref, o_ref, wo_ref, g_ref, wup_ref, wdn_ref, out_ref):
    x = x_ref[...] + _dot(o_ref[...], wo_ref[...])
    h = (x * _rms_scale(x) * g_ref[...]).astype(bf16)
    acc = x
    for c in range(D_FF // FF_TILE):
        cols = slice(c * FF_TILE, (c + 1) * FF_TILE)
        u = jnp.square(jnp.maximum(_dot(h, wup_ref[:, cols]), 0.0))
        acc = acc + _dot(u.astype(bf16), wdn_ref[cols, :])
    out_ref[...] = acc


def _proj_mlp(x, o, wo, gain, wup, wdn):
    n = x.shape[0]
    const = lambda i: (0, 0)
    rows = lambda i: (i, 0)
    return pl.pallas_call(
        _proj_mlp_kernel,
        name="proj_mlp",
        grid=(n // TOKEN_TILE,),
        in_specs=[
            pl.BlockSpec((TOKEN_TILE, D_MODEL), rows),
            pl.BlockSpec((TOKEN_TILE, D_MODEL), rows),
            pl.BlockSpec((D_MODEL, D_MODEL), const, pipeline_mode=pl.Buffered(1)),
            pl.BlockSpec((1, D_MODEL), const),
            pl.BlockSpec((D_MODEL, D_FF), const, pipeline_mode=pl.Buffered(1)),
            pl.BlockSpec((D_FF, D_MODEL), const, pipeline_mode=pl.Buffered(1)),
        ],
        out_specs=pl.BlockSpec((TOKEN_TILE, D_MODEL), rows),
        out_shape=jax.ShapeDtypeStruct((n, D_MODEL), f32),
        compiler_params=pltpu.CompilerParams(
            dimension_semantics=("parallel",), vmem_limit_bytes=VMEM_LIMIT),
    )(x, o, wo, gain, wup, wdn)


def kernel(x, mix_norm, w_qkv, w_o, q_norm, k_norm, rel_bias, ffn_norm, w_up, w_down):
    b, s, d = x.shape
    depth = w_qkv.shape[0]
    x = x.reshape(b * s, d)
    unit_gain = jnp.ones((1, LANES), f32)
    for layer in range(depth):
        chunked = layer % 2 == 1
        idx = layer // 2
        if chunked:
            q_gain = jnp.tile(q_norm[idx], HEADS_PER_TILE)[None]
            k_gain = jnp.tile(k_norm[idx], HEADS_PER_TILE)[None]
        else:
            q_gain = k_gain = unit_gain
        q, kcat, vcat = _norm_qkv(x, mix_norm[layer][None], w_qkv[layer].astype(bf16),
                                  q_gain, k_gain, chunked)
        q = q.reshape(b, s, d)
        kcat = kcat.reshape((b, s // KEY_TILE) + kcat.shape[1:])
        vcat = vcat.reshape((b, s // KEY_TILE) + vcat.shape[1:])
        if chunked:
            o = _chunk_attention(q, kcat, vcat, rel_bias[idx])
        else:
            o = _sb_attention(q, kcat, vcat)
        x = _proj_mlp(x, o.reshape(b * s, d), w_o[layer].astype(bf16), ffn_norm[layer][None],
                      w_up[layer].astype(bf16), w_down[layer].astype(bf16))
    return x.reshape(b, s, d)
```

```python
import functools

import jax
import jax.numpy as jnp
from jax import lax
from jax.experimental import pallas as pl
from jax.experimental.pallas import tpu as pltpu

D_MODEL = 1024
N_HEADS = 16
HEAD_DIM = D_MODEL // N_HEADS
D_FF = 4 * D_MODEL
CHUNK = 64
LEFT_CHUNKS = 8
MAX_REL = 256
RMS_EPS = 1e-6

LANES = 128
HEADS_PER_TILE = LANES // HEAD_DIM
N_PAIRS = N_HEADS // HEADS_PER_TILE
ATTN_SCALE = HEAD_DIM ** -0.5

TOKEN_TILE = 512
FF_TILE = 1024
KEY_TILE = 128
SB_TILE = KEY_TILE
SB_STEP_TILES = 16
SB_FAST_TILES = 3
CA_TILE = KEY_TILE
CA_KEY_TILES = LEFT_CHUNKS * CHUNK // CA_TILE + 1
CA_STEP_TILES = 16
CA_BAND = CA_KEY_TILES * CA_TILE
CA_ROLL_WIDTH = 1024
MASK_VALUE = -1e30

SB_DECAY_CUTOFF = 105.0
SB_DECAY_DONE = 1e30
LOG2_E = 1.4426950408889634

VMEM_LIMIT = 56 * 1024 * 1024

f32 = jnp.float32
bf16 = jnp.bfloat16


def _rms_scale(x):
    return lax.rsqrt(jnp.mean(x * x, axis=-1, keepdims=True) + RMS_EPS)


def _split_bf16(x):
    hi = x.astype(bf16)
    lo = (x - hi.astype(f32)).astype(bf16)
    return hi, lo


def _dot(a, b):
    return jnp.dot(a, b, preferred_element_type=f32)


def _dot_nt(a, b):
    return lax.dot_general(a, b, (((1,), (1,)), ((), ())), preferred_element_type=f32)


def _norm_qkv_kernel(x_ref, g_ref, w_ref, qg_ref, kg_ref, q_ref, kcat_ref, vcat_ref, *, head_norm):
    t = KEY_TILE
    x = x_ref[...]
    h = (x * _rms_scale(x) * g_ref[...]).astype(bf16)
    lane = lax.broadcasted_iota(jnp.int32, (t, LANES), 1)
    own = [lane // HEAD_DIM == head for head in range(HEADS_PER_TILE)]
    if head_norm:
        r = lax.broadcasted_iota(jnp.int32, (2 * LANES, LANES), 0) % LANES // HEAD_DIM
        c = lax.broadcasted_iota(jnp.int32, (2 * LANES, LANES), 1) // HEAD_DIM
        head_mean = jnp.where(r == c, 1.0 / HEAD_DIM, 0.0).astype(bf16)
    for part, cat_ref in enumerate((None, kcat_ref, vcat_ref)):
        acc = _dot(h, w_ref[:, part * D_MODEL:(part + 1) * D_MODEL])
        for p in range(N_PAIRS):
            blk = acc[:, p * LANES:(p + 1) * LANES]
            if head_norm and part < 2:
                hi, lo = _split_bf16(blk * blk)
                ms = _dot(jnp.concatenate([hi, lo], axis=1), head_mean)
                blk = blk * lax.rsqrt(ms + RMS_EPS) * (qg_ref if part == 0 else kg_ref)[...]
            blk = blk.astype(bf16)
            if cat_ref is None:
                q_ref[:, p * LANES:(p + 1) * LANES] = blk
                continue
            for tile in range(TOKEN_TILE // t):
                rows = blk[tile * t:(tile + 1) * t]
                for head in range(HEADS_PER_TILE):
                    cat_ref[tile, p, head * t:(head + 1) * t, :] = jnp.where(
                        own[head], rows, jnp.zeros_like(rows))


def _norm_qkv(x, gain, w, q_gain, k_gain, head_norm):
    n = x.shape[0]
    const = lambda i: (0, 0)
    cat_block = (TOKEN_TILE // KEY_TILE, N_PAIRS, HEADS_PER_TILE * KEY_TILE, LANES)
    cat_shape = jax.ShapeDtypeStruct((n // KEY_TILE,) + cat_block[1:], bf16)
    cat_spec = pl.BlockSpec(cat_block, lambda i: (i, 0, 0, 0))
    return pl.pallas_call(
        functools.partial(_norm_qkv_kernel, head_norm=head_norm),
        name="norm_qkv_hn" if head_norm else "norm_qkv",
        grid=(n // TOKEN_TILE,),
        in_specs=[
            pl.BlockSpec((TOKEN_TILE, D_MODEL), lambda i: (i, 0)),
            pl.BlockSpec((1, D_MODEL), const),
            pl.BlockSpec((D_MODEL, 3 * D_MODEL), const, pipeline_mode=pl.Buffered(1)),
            pl.BlockSpec((1, LANES), const),
            pl.BlockSpec((1, LANES), const),
        ],
        out_specs=[pl.BlockSpec((TOKEN_TILE, D_MODEL), lambda i: (i, 0)), cat_spec, cat_spec],
        out_shape=[jax.ShapeDtypeStruct((n, D_MODEL), bf16), cat_shape, cat_shape],
        compiler_params=pltpu.CompilerParams(
            dimension_semantics=("parallel",), vmem_limit_bytes=VMEM_LIMIT),
    )(x, gain, w, q_gain, k_gain)


def _split_trunc(x):
    hi = pltpu.bitcast(pltpu.bitcast(x, jnp.uint32) & jnp.uint32(0xFFFF0000), f32)
    return hi.astype(bf16), (x - hi).astype(bf16)


def _sb_attn_kernel(q_ref, kcat_ref, vcat_ref, o_ref, lb_ref, cum_ref, rest_ref, acc_ref):
    t = SB_TILE
    g = pl.program_id(2)
    heads = range(HEADS_PER_TILE)

    row = lax.broadcasted_iota(jnp.int32, (t, 2 * t), 0)
    col = lax.broadcasted_iota(jnp.int32, (t, 2 * t), 1)
    causal = col % t < row
    rk = lax.broadcasted_iota(jnp.int32, (2 * t, 2 * t), 0) % t
    ck = lax.broadcasted_iota(jnp.int32, (2 * t, 2 * t), 1)
    cum_mat = jnp.where((rk > ck) | (ck >= t), 1.0, 0.0).astype(bf16)

    queries = [q_ref[0, sub * t:(sub + 1) * t, :] * ATTN_SCALE for sub in range(SB_STEP_TILES)]

    def raw_scores(sub, j):
        return _dot_nt(queries[sub], kcat_ref[j])

    def scores(z, diagonal):
        decay = jnp.maximum(z, 0.0) + jnp.log(1.0 + jnp.exp2(jnp.abs(z) * -LOG2_E))
        if diagonal:
            decay = jnp.where(causal, decay, 0.0)
        cums = []
        for head in heads:
            hi, lo = _split_trunc(decay[:, head * t:(head + 1) * t])
            cums.append(_dot(jnp.concatenate([hi, lo], axis=1), cum_mat))
        return z - decay, cums

    def weights(log_beta, cums, rests, j, diagonal):
        logs = [log_beta[:, head * t:(head + 1) * t] - cums[head][:, :t] for head in heads]
        if rests is None:
            rests = [cums[head][:, t:] for head in heads]
        else:
            logs = [logs[head] - rests[head] for head in heads]
            rests = [rests[head] + cums[head][:, t:] for head in heads]
        a = jnp.exp(jnp.concatenate(logs, axis=1))
        if diagonal:
            a = jnp.where(causal, a, 0.0)
        return rests, _dot(a.astype(bf16), vcat_ref[j])

    def first_sweep(tile_of, depth_of):
        slots = [(sub, n) for sub in range(SB_STEP_TILES) for n in range(depth_of(sub))]
        zz = {}
        rests, acc = None, None

        def stage_scores(u):
            sub, n = slots[u]
            zz[u] = raw_scores(sub, tile_of(sub) - n)

        def stage_cumsum(u):
            log_beta, cums = scores(zz.pop(u), slots[u][1] == 0)
            lb_ref[u] = log_beta
            for head in heads:
                cum_ref[u, head] = cums[head]

        def stage_output(u):
            nonlocal rests, acc
            sub, n = slots[u]
            rests, out = weights(lb_ref[u], [cum_ref[u, head] for head in heads],
                                 None if n == 0 else rests, tile_of(sub) - n, n == 0)
            acc = out if n == 0 else acc + out
            if n == depth_of(sub) - 1:
                for head in heads:
                    rest_ref[sub, head] = rests[head]
                acc_ref[sub] = acc

        for step in range(len(slots) + 2):
            if step < len(slots):
                stage_scores(step)
            if 1 <= step <= len(slots):
                stage_cumsum(step - 1)
            if step >= 2:
                stage_output(step - 2)

    @pl.when(g == 0)
    def _():
        first_sweep(lambda sub: sub, lambda sub: min(SB_FAST_TILES, sub + 1))

    @pl.when(g > 0)
    def _():
        first_sweep(lambda sub: g * SB_STEP_TILES + sub, lambda sub: SB_FAST_TILES)

    def least_decay(rests):
        return jnp.min(functools.reduce(jnp.minimum, rests))

    def tile_rests(sub):
        return [rest_ref[sub, head] for head in heads]

    @pl.when(least_decay([r for sub in range(SB_STEP_TILES) for r in tile_rests(sub)])
             < SB_DECAY_CUTOFF)
    def _():
        def cond(state):
            return functools.reduce(jnp.minimum, state[1:]) < SB_DECAY_CUTOFF

        def body(state):
            n, leasts = state[0], state[1:]
            updated = []
            for sub in range(SB_STEP_TILES):
                j = g * SB_STEP_TILES + sub - n

                def visit(sub=sub, j=j):
                    j = jnp.maximum(j, 0)
                    log_beta, cums = scores(raw_scores(sub, j), False)
                    rests, out = weights(log_beta, cums, tile_rests(sub), j, False)
                    for head in heads:
                        rest_ref[sub, head] = rests[head]
                    acc_ref[sub] = acc_ref[sub] + out
                    return least_decay(rests)

                def skip(j=j, least=leasts[sub]):
                    return jnp.where(j >= 0, least, SB_DECAY_DONE)

                pending = jnp.logical_and(j >= 0, leasts[sub] < SB_DECAY_CUTOFF)
                updated.append(lax.cond(pending, visit, skip))
            return (n + 1, *updated)

        lax.while_loop(cond, body, (jnp.int32(SB_FAST_TILES),
                                    *[least_decay(tile_rests(sub)) for sub in range(SB_STEP_TILES)]))

    for sub in range(SB_STEP_TILES):
        o_ref[0, sub * t:(sub + 1) * t, :] = acc_ref[sub].astype(o_ref.dtype)


def _attention_specs(b, s, step):
    rows = pl.BlockSpec((1, step, LANES), lambda bi, p, i: (bi, i, p))
    cat = pl.BlockSpec((None, s // KEY_TILE, None, HEADS_PER_TILE * KEY_TILE, LANES),
                       lambda bi, p, i: (bi, 0, p, 0, 0))
    return (b, N_PAIRS, s // step), [rows, cat, cat], rows


def _sb_attention(q, kcat, vcat):
    b, s, _ = q.shape
    t = SB_TILE
    slots = SB_STEP_TILES * SB_FAST_TILES
    grid, in_specs, out_spec = _attention_specs(b, s, SB_STEP_TILES * t)
    return pl.pallas_call(
        _sb_attn_kernel,
        name="sb_attention",
        grid=grid,
        in_specs=in_specs,
        out_specs=out_spec,
        out_shape=jax.ShapeDtypeStruct((b, s, D_MODEL), bf16),
        scratch_shapes=[pltpu.VMEM((slots, t, HEADS_PER_TILE * t), f32),
                        pltpu.VMEM((slots, HEADS_PER_TILE, t, 2 * t), f32),
                        pltpu.VMEM((SB_STEP_TILES, HEADS_PER_TILE, t, t), f32),
                        pltpu.VMEM((SB_STEP_TILES, t, LANES), f32)],
        compiler_params=pltpu.CompilerParams(
            dimension_semantics=("parallel", "parallel", "parallel"),
            vmem_limit_bytes=VMEM_LIMIT),
    )(q, kcat, vcat)


def _chunk_bias_kernel(u_ref, tab_ref):
    t = CA_TILE
    pair = pl.program_id(0)
    r = lax.broadcasted_iota(jnp.int32, (t, t), 0)
    c = lax.broadcasted_iota(jnp.int32, (t, t), 1)
    for head in range(HEADS_PER_TILE):
        u = jnp.broadcast_to(u_ref[pl.ds(pair * HEADS_PER_TILE + head, 1), :], (t, CA_ROLL_WIDTH))
        skew = pltpu.roll(u, 0, 1, stride=1, stride_axis=0)
        for kt in range(CA_KEY_TILES):
            chunk_dist = r // CHUNK + LEFT_CHUNKS - (kt * t + c) // CHUNK
            in_band = (chunk_dist >= 0) & (chunk_dist <= LEFT_CHUNKS)
            tab_ref[0, kt, :, head * t:(head + 1) * t] = jnp.where(
                in_band, skew[:, kt * t:(kt + 1) * t], MASK_VALUE)


def _chunk_bias_table(rel_bias):
    h = rel_bias.shape[0]
    far = rel_bias[:, 2 * MAX_REL:]
    near = rel_bias[:, 2 * MAX_REL - (CA_BAND - MAX_REL) + 1:][:, ::-1]
    u = jnp.concatenate([jnp.broadcast_to(far, (h, CA_BAND - 2 * MAX_REL + CA_TILE)), near,
                         jnp.broadcast_to(far, (h, CA_ROLL_WIDTH - CA_BAND))], axis=1)
    tile = (CA_KEY_TILES, CA_TILE, HEADS_PER_TILE * CA_TILE)
    return pl.pallas_call(
        _chunk_bias_kernel,
        name="chunk_bias_table",
        grid=(h // HEADS_PER_TILE,),
        in_specs=[pl.BlockSpec((h, CA_ROLL_WIDTH), lambda i: (0, 0))],
        out_specs=pl.BlockSpec((1,) + tile, lambda i: (i, 0, 0, 0)),
        out_shape=jax.ShapeDtypeStruct((h // HEADS_PER_TILE,) + tile, f32),
    )(u.astype(f32))


def _chunk_attn_kernel(q_ref, kcat_ref, vcat_ref, bias_ref, o_ref, logit_ref):
    t = CA_TILE
    g = pl.program_id(2)
    heads = range(HEADS_PER_TILE)
    r = lax.broadcasted_iota(jnp.int32, (HEADS_PER_TILE * t, LANES), 0) // t
    c = lax.broadcasted_iota(jnp.int32, (HEADS_PER_TILE * t, LANES), 1) // HEAD_DIM
    head_sums = jnp.where(r == c, 1.0, 0.0).astype(bf16)

    def sweep(tile_of):
        logits = {}

        def stage_logits(sub):
            q = q_ref[0, sub * t:(sub + 1) * t, :] * ATTN_SCALE
            tiles = []
            for kt in range(CA_KEY_TILES):
                j = tile_of(sub) - (CA_KEY_TILES - 1) + kt
                if isinstance(j, int) and j < 0:
                    continue
                logit_ref[sub % 2, kt] = _dot_nt(q, kcat_ref[j]) + bias_ref[0, kt]
                tiles.append((j, kt))
            logits[sub] = tiles

        def stage_output(sub):
            tiles = [(j, logit_ref[sub % 2, kt]) for j, kt in logits.pop(sub)]
            top = functools.reduce(jnp.maximum, [s for _, s in tiles])
            tops = [jnp.max(top[:, head * t:(head + 1) * t], axis=-1, keepdims=True)
                    for head in heads]
            acc = None
            for j, s in tiles:
                p = jnp.concatenate([jnp.exp(s[:, head * t:(head + 1) * t] - tops[head])
                                     for head in heads], axis=1)
                values = jnp.concatenate([vcat_ref[j], head_sums], axis=1)
                out = _dot(p.astype(bf16), values)
                acc = out if acc is None else acc + out
            o_ref[0, sub * t:(sub + 1) * t, :] = (acc[:, :LANES] / acc[:, LANES:]).astype(o_ref.dtype)

        for step in range(CA_STEP_TILES + 1):
            if step < CA_STEP_TILES:
                stage_logits(step)
            if step >= 1:
                stage_output(step - 1)

    @pl.when(g == 0)
    def _():
        sweep(lambda sub: sub)

    @pl.when(g > 0)
    def _():
        sweep(lambda sub: g * CA_STEP_TILES + sub)


def _chunk_attention(q, kcat, vcat, rel_bias):
    b, s, _ = q.shape
    t = CA_TILE
    table = _chunk_bias_table(rel_bias)
    grid, in_specs, out_spec = _attention_specs(b, s, CA_STEP_TILES * t)
    bias_spec = pl.BlockSpec((1, CA_KEY_TILES, t, HEADS_PER_TILE * t),
                             lambda bi, p, i: (p, 0, 0, 0))
    return pl.pallas_call(
        _chunk_attn_kernel,
        name="chunk_attention",
        grid=grid,
        in_specs=in_specs + [bias_spec],
        out_specs=out_spec,
        out_shape=jax.ShapeDtypeStruct((b, s, D_MODEL), bf16),
        scratch_shapes=[pltpu.VMEM((2, CA_KEY_TILES, t, HEADS_PER_TILE * t), f32)],
        compiler_params=pltpu.CompilerParams(
            dimension_semantics=("parallel", "parallel", "parallel"),
            vmem_limit_bytes=VMEM_LIMIT),
    )(q, kcat, vcat, table)


def _proj_mlp_kernel(x_ref, o_ref, wo_ref, g_ref, wup_ref, wdn_ref, out_ref):
    x = x_ref[...] + _dot(o_ref[...], wo_ref[...])
    h = (x * _rms_scale(x) * g_ref[...]).astype(bf16)
    acc = x
    for c in range(D_FF // FF_TILE):
        cols = slice(c * FF_TILE, (c + 1) * FF_TILE)
        u = jnp.square(jnp.maximum(_dot(h, wup_ref[:, cols]), 0.0))
        acc = acc + _dot(u.astype(bf16), wdn_ref[cols, :])
    out_ref[...] = acc


def _proj_mlp(x, o, wo, gain, wup, wdn):
    n = x.shape[0]
    const = lambda i: (0, 0)
    rows = lambda i: (i, 0)
    return pl.pallas_call(
        _proj_mlp_kernel,
        name="proj_mlp",
        grid=(n // TOKEN_TILE,),
        in_specs=[
            pl.BlockSpec((TOKEN_TILE, D_MODEL), rows),
            pl.BlockSpec((TOKEN_TILE, D_MODEL), rows),
            pl.BlockSpec((D_MODEL, D_MODEL), const, pipeline_mode=pl.Buffered(1)),
            pl.BlockSpec((1, D_MODEL), const),
            pl.BlockSpec((D_MODEL, D_FF), const, pipeline_mode=pl.Buffered(1)),
            pl.BlockSpec((D_FF, D_MODEL), const, pipeline_mode=pl.Buffered(1)),
        ],
        out_specs=pl.BlockSpec((TOKEN_TILE, D_MODEL), rows),
        out_shape=jax.ShapeDtypeStruct((n, D_MODEL), f32),
        compiler_params=pltpu.CompilerParams(
            dimension_semantics=("parallel",), vmem_limit_bytes=VMEM_LIMIT),
    )(x, o, wo, gain, wup, wdn)


def kernel(x, mix_norm, w_qkv, w_o, q_norm, k_norm, rel_bias, ffn_norm, w_up, w_down):
    b, s, d = x.shape
    depth = w_qkv.shape[0]
    x = x.reshape(b * s, d)
    unit_gain = jnp.ones((1, LANES), f32)
    for layer in range(depth):
        chunked = layer % 2 == 1
        idx = layer // 2
        if chunked:
            q_gain = jnp.tile(q_norm[idx], HEADS_PER_TILE)[None]
            k_gain = jnp.tile(k_norm[idx], HEADS_PER_TILE)[None]
        else:
            q_gain = k_gain = unit_gain
        q, kcat, vcat = _norm_qkv(x, mix_norm[layer][None], w_qkv[layer].astype(bf16),
                                  q_gain, k_gain, chunked)
        q = q.reshape(b, s, d)
        kcat = kcat.reshape((b, s // KEY_TILE) + kcat.shape[1:])
        vcat = vcat.reshape((b, s // KEY_TILE) + vcat.shape[1:])
        if chunked:
            o = _chunk_attention(q, kcat, vcat, rel_bias[idx])
        else:
            o = _sb_attention(q, kcat, vcat)
        x = _proj_mlp(x, o.reshape(b * s, d), w_o[layer].astype(bf16), ffn_norm[layer][None],
                      w_up[layer].astype(bf16), w_down[layer].astype(bf16))
    return x.reshape(b, s, d)
```

```python
import functools

import jax
import jax.numpy as jnp
from jax import lax
from jax.experimental import pallas as pl
from jax.experimental.pallas import tpu as pltpu

D_MODEL = 1024
N_HEADS = 16
HEAD_DIM = D_MODEL // N_HEADS
D_FF = 4 * D_MODEL
CHUNK = 64
LEFT_CHUNKS = 8
MAX_REL = 256
RMS_EPS = 1e-6

LANES = 128
HEADS_PER_TILE = LANES // HEAD_DIM
N_PAIRS = N_HEADS // HEADS_PER_TILE
ATTN_SCALE = HEAD_DIM ** -0.5

TOKEN_TILE = 512
FF_TILE = 1024
KEY_TILE = 128
SB_TILE = KEY_TILE
SB_STEP_TILES = 16
SB_FAST_TILES = 3
CA_TILE = KEY_TILE
CA_KEY_TILES = LEFT_CHUNKS * CHUNK // CA_TILE + 1
CA_STEP_TILES = 32
CA_BAND = CA_KEY_TILES * CA_TILE
CA_ROLL_WIDTH = 1024
MASK_VALUE = -1e30

SB_DECAY_CUTOFF = 105.0
SB_DECAY_DONE = 1e30
LOG2_E = 1.4426950408889634

VMEM_LIMIT = 56 * 1024 * 1024

f32 = jnp.float32
bf16 = jnp.bfloat16


def _rms_scale(x):
    return lax.rsqrt(jnp.mean(x * x, axis=-1, keepdims=True) + RMS_EPS)


def _split_bf16(x):
    hi = x.astype(bf16)
    lo = (x - hi.astype(f32)).astype(bf16)
    return hi, lo


def _dot(a, b):
    return jnp.dot(a, b, preferred_element_type=f32)


def _dot_nt(a, b):
    return lax.dot_general(a, b, (((1,), (1,)), ((), ())), preferred_element_type=f32)


def _norm_qkv_kernel(x_ref, g_ref, w_ref, qg_ref, kg_ref, q_ref, kcat_ref, vcat_ref, *, head_norm):
    t = KEY_TILE
    x = x_ref[...]
    h = (x * _rms_scale(x) * g_ref[...]).astype(bf16)
    lane = lax.broadcasted_iota(jnp.int32, (t, LANES), 1)
    own = [lane // HEAD_DIM == head for head in range(HEADS_PER_TILE)]
    if head_norm:
        r = lax.broadcasted_iota(jnp.int32, (2 * LANES, LANES), 0) % LANES // HEAD_DIM
        c = lax.broadcasted_iota(jnp.int32, (2 * LANES, LANES), 1) // HEAD_DIM
        head_mean = jnp.where(r == c, 1.0 / HEAD_DIM, 0.0).astype(bf16)
    for part, cat_ref in enumerate((None, kcat_ref, vcat_ref)):
        acc = _dot(h, w_ref[:, part * D_MODEL:(part + 1) * D_MODEL])
        for p in range(N_PAIRS):
            blk = acc[:, p * LANES:(p + 1) * LANES]
            if head_norm and part < 2:
                hi, lo = _split_bf16(blk * blk)
                ms = _dot(jnp.concatenate([hi, lo], axis=1), head_mean)
                blk = blk * lax.rsqrt(ms + RMS_EPS) * (qg_ref if part == 0 else kg_ref)[...]
            blk = blk.astype(bf16)
            if cat_ref is None:
                q_ref[:, p * LANES:(p + 1) * LANES] = blk
                continue
            for tile in range(TOKEN_TILE // t):
                rows = blk[tile * t:(tile + 1) * t]
                for head in range(HEADS_PER_TILE):
                    cat_ref[tile, p, head * t:(head + 1) * t, :] = jnp.where(
                        own[head], rows, jnp.zeros_like(rows))


def _layer_weight(w, layer):
    return pl.BlockSpec((None,) + w.shape[1:], lambda i: (layer, 0, 0),
                        pipeline_mode=pl.Buffered(1))


def _norm_qkv(x, gain, w, layer, q_gain, k_gain, head_norm):
    n = x.shape[0]
    const = lambda i: (0, 0)
    cat_block = (TOKEN_TILE // KEY_TILE, N_PAIRS, HEADS_PER_TILE * KEY_TILE, LANES)
    cat_shape = jax.ShapeDtypeStruct((n // KEY_TILE,) + cat_block[1:], bf16)
    cat_spec = pl.BlockSpec(cat_block, lambda i: (i, 0, 0, 0))
    return pl.pallas_call(
        functools.partial(_norm_qkv_kernel, head_norm=head_norm),
        name="norm_qkv_hn" if head_norm else "norm_qkv",
        grid=(n // TOKEN_TILE,),
        in_specs=[
            pl.BlockSpec((TOKEN_TILE, D_MODEL), lambda i: (i, 0)),
            pl.BlockSpec((1, D_MODEL), const),
            _layer_weight(w, layer),
            pl.BlockSpec((1, LANES), const),
            pl.BlockSpec((1, LANES), const),
        ],
        out_specs=[pl.BlockSpec((TOKEN_TILE, D_MODEL), lambda i: (i, 0)), cat_spec, cat_spec],
        out_shape=[jax.ShapeDtypeStruct((n, D_MODEL), bf16), cat_shape, cat_shape],
        compiler_params=pltpu.CompilerParams(
            dimension_semantics=("parallel",), vmem_limit_bytes=VMEM_LIMIT),
    )(x, gain, w, q_gain, k_gain)


def _split_trunc(x):
    hi = pltpu.bitcast(pltpu.bitcast(x, jnp.uint32) & jnp.uint32(0xFFFF0000), f32)
    return hi.astype(bf16), (x - hi).astype(bf16)


def _sb_attn_kernel(q_ref, kcat_ref, vcat_ref, o_ref, lb_ref, cum_ref, rest_ref, acc_ref):
    t = SB_TILE
    g = pl.program_id(2)
    heads = range(HEADS_PER_TILE)

    row = lax.broadcasted_iota(jnp.int32, (t, 2 * t), 0)
    col = lax.broadcasted_iota(jnp.int32, (t, 2 * t), 1)
    causal = col % t < row
    rk = lax.broadcasted_iota(jnp.int32, (2 * t, 2 * t), 0) % t
    ck = lax.broadcasted_iota(jnp.int32, (2 * t, 2 * t), 1)
    cum_mat = jnp.where((rk > ck) | (ck >= t), 1.0, 0.0).astype(bf16)

    queries = [q_ref[0, sub * t:(sub + 1) * t, :] * ATTN_SCALE for sub in range(SB_STEP_TILES)]

    def raw_scores(sub, j):
        return _dot_nt(queries[sub], kcat_ref[j])

    def scores(z, diagonal):
        decay = jnp.maximum(z, 0.0) + jnp.log(1.0 + jnp.exp2(jnp.abs(z) * -LOG2_E))
        if diagonal:
            decay = jnp.where(causal, decay, 0.0)
        cums = []
        for head in heads:
            hi, lo = _split_trunc(decay[:, head * t:(head + 1) * t])
            cums.append(_dot(jnp.concatenate([hi, lo], axis=1), cum_mat))
        return z - decay, cums

    def weights(log_beta, cums, rests, j, diagonal):
        logs = [log_beta[:, head * t:(head + 1) * t] - cums[head][:, :t] for head in heads]
        if rests is None:
            rests = [cums[head][:, t:] for head in heads]
        else:
            logs = [logs[head] - rests[head] for head in heads]
            rests = [rests[head] + cums[head][:, t:] for head in heads]
        a = jnp.exp(jnp.concatenate(logs, axis=1))
        if diagonal:
            a = jnp.where(causal, a, 0.0)
        return rests, _dot(a.astype(bf16), vcat_ref[j])

    def first_sweep(tile_of, depth_of):
        slots = [(sub, n) for sub in range(SB_STEP_TILES) for n in range(depth_of(sub))]
        zz = {}
        rests, acc = None, None

        def stage_scores(u):
            sub, n = slots[u]
            zz[u] = raw_scores(sub, tile_of(sub) - n)

        def stage_cumsum(u):
            log_beta, cums = scores(zz.pop(u), slots[u][1] == 0)
            lb_ref[u] = log_beta
            for head in heads:
                cum_ref[u, head] = cums[head]

        def stage_output(u):
            nonlocal rests, acc
            sub, n = slots[u]
            rests, out = weights(lb_ref[u], [cum_ref[u, head] for head in heads],
                                 None if n == 0 else rests, tile_of(sub) - n, n == 0)
            acc = out if n == 0 else acc + out
            if n == depth_of(sub) - 1:
                for head in heads:
                    rest_ref[sub, head] = rests[head]
                acc_ref[sub] = acc

        for step in range(len(slots) + 2):
            if step < len(slots):
                stage_scores(step)
            if 1 <= step <= len(slots):
                stage_cumsum(step - 1)
            if step >= 2:
                stage_output(step - 2)

    @pl.when(g == 0)
    def _():
        first_sweep(lambda sub: sub, lambda sub: min(SB_FAST_TILES, sub + 1))

    @pl.when(g > 0)
    def _():
        first_sweep(lambda sub: g * SB_STEP_TILES + sub, lambda sub: SB_FAST_TILES)

    def least_decay(rests):
        return jnp.min(functools.reduce(jnp.minimum, rests))

    def tile_rests(sub):
        return [rest_ref[sub, head] for head in heads]

    @pl.when(least_decay([r for sub in range(SB_STEP_TILES) for r in tile_rests(sub)])
             < SB_DECAY_CUTOFF)
    def _():
        def cond(state):
            return functools.reduce(jnp.minimum, state[1:]) < SB_DECAY_CUTOFF

        def body(state):
            n, leasts = state[0], state[1:]
            updated = []
            for sub in range(SB_STEP_TILES):
                j = g * SB_STEP_TILES + sub - n

                def visit(sub=sub, j=j):
                    j = jnp.maximum(j, 0)
                    log_beta, cums = scores(raw_scores(sub, j), False)
                    rests, out = weights(log_beta, cums, tile_rests(sub), j, False)
                    for head in heads:
                        rest_ref[sub, head] = rests[head]
                    acc_ref[sub] = acc_ref[sub] + out
                    return least_decay(rests)

                def skip(j=j, least=leasts[sub]):
                    return jnp.where(j >= 0, least, SB_DECAY_DONE)

                pending = jnp.logical_and(j >= 0, leasts[sub] < SB_DECAY_CUTOFF)
                updated.append(lax.cond(pending, visit, skip))
            return (n + 1, *updated)

        lax.while_loop(cond, body, (jnp.int32(SB_FAST_TILES),
                                    *[least_decay(tile_rests(sub)) for sub in range(SB_STEP_TILES)]))

    for sub in range(SB_STEP_TILES):
        o_ref[0, sub * t:(sub + 1) * t, :] = acc_ref[sub].astype(o_ref.dtype)


def _attention_specs(b, s, step):
    rows = pl.BlockSpec((1, step, LANES), lambda bi, p, i: (bi, i, p))
    cat = pl.BlockSpec((None, s // KEY_TILE, None, HEADS_PER_TILE * KEY_TILE, LANES),
                       lambda bi, p, i: (bi, 0, p, 0, 0))
    return (b, N_PAIRS, s // step), [rows, cat, cat], rows


def _sb_attention(q, kcat, vcat):
    b, s, _ = q.shape
    t = SB_TILE
    slots = SB_STEP_TILES * SB_FAST_TILES
    grid, in_specs, out_spec = _attention_specs(b, s, SB_STEP_TILES * t)
    return pl.pallas_call(
        _sb_attn_kernel,
        name="sb_attention",
        grid=grid,
        in_specs=in_specs,
        out_specs=out_spec,
        out_shape=jax.ShapeDtypeStruct((b, s, D_MODEL), bf16),
        scratch_shapes=[pltpu.VMEM((slots, t, HEADS_PER_TILE * t), f32),
                        pltpu.VMEM((slots, HEADS_PER_TILE, t, 2 * t), f32),
                        pltpu.VMEM((SB_STEP_TILES, HEADS_PER_TILE, t, t), f32),
                        pltpu.VMEM((SB_STEP_TILES, t, LANES), f32)],
        compiler_params=pltpu.CompilerParams(
            dimension_semantics=("parallel", "parallel", "parallel"),
            vmem_limit_bytes=VMEM_LIMIT),
    )(q, kcat, vcat)


def _chunk_bias_kernel(u_ref, tab_ref):
    t = CA_TILE
    pair = pl.program_id(0)
    r = lax.broadcasted_iota(jnp.int32, (t, t), 0)
    c = lax.broadcasted_iota(jnp.int32, (t, t), 1)
    for head in range(HEADS_PER_TILE):
        u = jnp.broadcast_to(u_ref[pl.ds(pair * HEADS_PER_TILE + head, 1), :], (t, CA_ROLL_WIDTH))
        skew = pltpu.roll(u, 0, 1, stride=1, stride_axis=0)
        for kt in range(CA_KEY_TILES):
            chunk_dist = r // CHUNK + LEFT_CHUNKS - (kt * t + c) // CHUNK
            in_band = (chunk_dist >= 0) & (chunk_dist <= LEFT_CHUNKS)
            tab_ref[0, kt, :, head * t:(head + 1) * t] = jnp.where(
                in_band, skew[:, kt * t:(kt + 1) * t], MASK_VALUE)


def _chunk_bias_table(rel_bias):
    h = rel_bias.shape[0]
    far = rel_bias[:, 2 * MAX_REL:]
    near = rel_bias[:, 2 * MAX_REL - (CA_BAND - MAX_REL) + 1:][:, ::-1]
    u = jnp.concatenate([jnp.broadcast_to(far, (h, CA_BAND - 2 * MAX_REL + CA_TILE)), near,
                         jnp.broadcast_to(far, (h, CA_ROLL_WIDTH - CA_BAND))], axis=1)
    tile = (CA_KEY_TILES, CA_TILE, HEADS_PER_TILE * CA_TILE)
    return pl.pallas_call(
        _chunk_bias_kernel,
        name="chunk_bias_table",
        grid=(h // HEADS_PER_TILE,),
        in_specs=[pl.BlockSpec((h, CA_ROLL_WIDTH), lambda i: (0, 0))],
        out_specs=pl.BlockSpec((1,) + tile, lambda i: (i, 0, 0, 0)),
        out_shape=jax.ShapeDtypeStruct((h // HEADS_PER_TILE,) + tile, f32),
    )(u.astype(f32))


def _chunk_attn_kernel(q_ref, kcat_ref, vcat_ref, bias_ref, o_ref, logit_ref):
    t = CA_TILE
    g = pl.program_id(2)
    heads = range(HEADS_PER_TILE)
    r = lax.broadcasted_iota(jnp.int32, (HEADS_PER_TILE * t, LANES), 0) // t
    c = lax.broadcasted_iota(jnp.int32, (HEADS_PER_TILE * t, LANES), 1) // HEAD_DIM
    head_sums = jnp.where(r == c, 1.0, 0.0).astype(bf16)

    def sweep(tile_of):
        logits = {}

        def stage_logits(sub):
            q = q_ref[0, sub * t:(sub + 1) * t, :] * ATTN_SCALE
            tiles = []
            for kt in range(CA_KEY_TILES):
                j = tile_of(sub) - (CA_KEY_TILES - 1) + kt
                if isinstance(j, int) and j < 0:
                    continue
                logit_ref[sub % 2, kt] = _dot_nt(q, kcat_ref[j]) + bias_ref[0, kt]
                tiles.append((j, kt))
            logits[sub] = tiles

        def stage_output(sub):
            tiles = [(j, logit_ref[sub % 2, kt]) for j, kt in logits.pop(sub)]
            top = functools.reduce(jnp.maximum, [s for _, s in tiles])
            tops = [jnp.max(top[:, head * t:(head + 1) * t], axis=-1, keepdims=True)
                    for head in heads]
            acc = None
            for j, s in tiles:
                p = jnp.concatenate([jnp.exp(s[:, head * t:(head + 1) * t] - tops[head])
                                     for head in heads], axis=1)
                values = jnp.concatenate([vcat_ref[j], head_sums], axis=1)
                out = _dot(p.astype(bf16), values)
                acc = out if acc is None else acc + out
            o_ref[0, sub * t:(sub + 1) * t, :] = (acc[:, :LANES] / acc[:, LANES:]).astype(o_ref.dtype)

        for step in range(CA_STEP_TILES + 1):
            if step < CA_STEP_TILES:
                stage_logits(step)
            if step >= 1:
                stage_output(step - 1)

    @pl.when(g == 0)
    def _():
        sweep(lambda sub: sub)

    @pl.when(g > 0)
    def _():
        sweep(lambda sub: g * CA_STEP_TILES + sub)


def _chunk_attention(q, kcat, vcat, rel_bias):
    b, s, _ = q.shape
    t = CA_TILE
    table = _chunk_bias_table(rel_bias)
    grid, in_specs, out_spec = _attention_specs(b, s, CA_STEP_TILES * t)
    bias_spec = pl.BlockSpec((1, CA_KEY_TILES, t, HEADS_PER_TILE * t),
                             lambda bi, p, i: (p, 0, 0, 0))
    return pl.pallas_call(
        _chunk_attn_kernel,
        name="chunk_attention",
        grid=grid,
        in_specs=in_specs + [bias_spec],
        out_specs=out_spec,
        out_shape=jax.ShapeDtypeStruct((b, s, D_MODEL), bf16),
        scratch_shapes=[pltpu.VMEM((2, CA_KEY_TILES, t, HEADS_PER_TILE * t), f32)],
        compiler_params=pltpu.CompilerParams(
            dimension_semantics=("parallel", "parallel", "parallel"),
            vmem_limit_bytes=VMEM_LIMIT),
    )(q, kcat, vcat, table)


def _proj_mlp_kernel(x_ref, o_ref, wo_ref, g_ref, wup_ref, wdn_ref, out_ref):
    x = x_ref[...] + _dot(o_ref[...], wo_ref[...])
    h = (x * _rms_scale(x) * g_ref[...]).astype(bf16)
    acc = x
    for c in range(D_FF // FF_TILE):
        cols = slice(c * FF_TILE, (c + 1) * FF_TILE)
        u = jnp.square(jnp.maximum(_dot(h, wup_ref[:, cols]), 0.0))
        acc = acc + _dot(u.astype(bf16), wdn_ref[cols, :])
    out_ref[...] = acc


def _proj_mlp(x, o, wo, gain, wup, wdn, layer):
    n = x.shape[0]
    const = lambda i: (0, 0)
    rows = lambda i: (i, 0)
    return pl.pallas_call(
        _proj_mlp_kernel,
        name="proj_mlp",
        grid=(n // TOKEN_TILE,),
        in_specs=[
            pl.BlockSpec((TOKEN_TILE, D_MODEL), rows),
            pl.BlockSpec((TOKEN_TILE, D_MODEL), rows),
            _layer_weight(wo, layer),
            pl.BlockSpec((1, D_MODEL), const),
            _layer_weight(wup, layer),
            _layer_weight(wdn, layer),
        ],
        out_specs=pl.BlockSpec((TOKEN_TILE, D_MODEL), rows),
        out_shape=jax.ShapeDtypeStruct((n, D_MODEL), f32),
        compiler_params=pltpu.CompilerParams(
            dimension_semantics=("parallel",), vmem_limit_bytes=VMEM_LIMIT),
    )(x, o, wo, gain, wup, wdn)


def kernel(x, mix_norm, w_qkv, w_o, q_norm, k_norm, rel_bias, ffn_norm, w_up, w_down):
    b, s, d = x.shape
    depth = w_qkv.shape[0]
    x = x.reshape(b * s, d)
    unit_gain = jnp.ones((1, LANES), f32)
    w_qkv, w_o, w_up, w_down = (w.astype(bf16) for w in (w_qkv, w_o, w_up, w_down))
    for layer in range(depth):
        chunked = layer % 2 == 1
        idx = layer // 2
        if chunked:
            q_gain = jnp.tile(q_norm[idx], HEADS_PER_TILE)[None]
            k_gain = jnp.tile(k_norm[idx], HEADS_PER_TILE)[None]
        else:
            q_gain = k_gain = unit_gain
        q, kcat, vcat = _norm_qkv(x, mix_norm[layer][None], w_qkv, layer, q_gain, k_gain, chunked)
        q = q.reshape(b, s, d)
        kcat = kcat.reshape((b, s // KEY_TILE) + kcat.shape[1:])
        vcat = vcat.reshape((b, s // KEY_TILE) + vcat.shape[1:])
        if chunked:
            o = _chunk_attention(q, kcat, vcat, rel_bias[idx])
        else:
            o = _sb_attention(q, kcat, vcat)
        x = _proj_mlp(x, o.reshape(b * s, d), w_o, ffn_norm[layer][None], w_up, w_down, layer)
    return x.reshape(b, s, d)
```

```python
import functools

import jax
import jax.numpy as jnp
from jax import lax
from jax.experimental import pallas as pl
from jax.experimental.pallas import tpu as pltpu

D_MODEL = 1024
N_HEADS = 16
HEAD_DIM = D_MODEL // N_HEADS
D_FF = 4 * D_MODEL
CHUNK = 64
LEFT_CHUNKS = 8
MAX_REL = 256
RMS_EPS = 1e-6

LANES = 128
HEADS_PER_TILE = LANES // HEAD_DIM
N_PAIRS = N_HEADS // HEADS_PER_TILE
ATTN_SCALE = HEAD_DIM ** -0.5

TOKEN_TILE = 1024
FF_TILE = 1024
KEY_TILE = 128
SB_TILE = KEY_TILE
SB_STEP_TILES = 16
SB_FAST_TILES = 3
CA_TILE = KEY_TILE
CA_KEY_TILES = LEFT_CHUNKS * CHUNK // CA_TILE + 1
CA_STEP_TILES = 32
CA_BAND = CA_KEY_TILES * CA_TILE
CA_ROLL_WIDTH = 1024
MASK_VALUE = -1e30

SB_DECAY_CUTOFF = 105.0
SB_DECAY_DONE = 1e30
LOG2_E = 1.4426950408889634

VMEM_LIMIT = 56 * 1024 * 1024

f32 = jnp.float32
bf16 = jnp.bfloat16


def _rms_scale(x):
    return lax.rsqrt(jnp.mean(x * x, axis=-1, keepdims=True) + RMS_EPS)


def _split_bf16(x):
    hi = x.astype(bf16)
    lo = (x - hi.astype(f32)).astype(bf16)
    return hi, lo


def _dot(a, b):
    return jnp.dot(a, b, preferred_element_type=f32)


def _dot_nt(a, b):
    return lax.dot_general(a, b, (((1,), (1,)), ((), ())), preferred_element_type=f32)


def _norm_qkv_kernel(x_ref, g_ref, w_ref, qg_ref, kg_ref, q_ref, kcat_ref, vcat_ref, *, head_norm):
    t = KEY_TILE
    x = x_ref[...]
    h = (x * _rms_scale(x) * g_ref[...]).astype(bf16)
    lane = lax.broadcasted_iota(jnp.int32, (t, LANES), 1)
    own = [lane // HEAD_DIM == head for head in range(HEADS_PER_TILE)]
    if head_norm:
        r = lax.broadcasted_iota(jnp.int32, (2 * LANES, LANES), 0) % LANES // HEAD_DIM
        c = lax.broadcasted_iota(jnp.int32, (2 * LANES, LANES), 1) // HEAD_DIM
        head_mean = jnp.where(r == c, 1.0 / HEAD_DIM, 0.0).astype(bf16)
    for part, cat_ref in enumerate((None, kcat_ref, vcat_ref)):
        acc = _dot(h, w_ref[:, part * D_MODEL:(part + 1) * D_MODEL])
        for p in range(N_PAIRS):
            blk = acc[:, p * LANES:(p + 1) * LANES]
            if head_norm and part < 2:
                hi, lo = _split_bf16(blk * blk)
                ms = _dot(jnp.concatenate([hi, lo], axis=1), head_mean)
                blk = blk * lax.rsqrt(ms + RMS_EPS) * (qg_ref if part == 0 else kg_ref)[...]
            blk = blk.astype(bf16)
            if cat_ref is None:
                q_ref[:, p * LANES:(p + 1) * LANES] = blk
                continue
            for tile in range(TOKEN_TILE // t):
                rows = blk[tile * t:(tile + 1) * t]
                for head in range(HEADS_PER_TILE):
                    cat_ref[tile, p, head * t:(head + 1) * t, :] = jnp.where(
                        own[head], rows, jnp.zeros_like(rows))


def _layer_weight(w, layer):
    return pl.BlockSpec((None,) + w.shape[1:], lambda i: (layer, 0, 0),
                        pipeline_mode=pl.Buffered(1))


def _norm_qkv(x, gain, w, layer, q_gain, k_gain, head_norm):
    n = x.shape[0]
    const = lambda i: (0, 0)
    cat_block = (TOKEN_TILE // KEY_TILE, N_PAIRS, HEADS_PER_TILE * KEY_TILE, LANES)
    cat_shape = jax.ShapeDtypeStruct((n // KEY_TILE,) + cat_block[1:], bf16)
    cat_spec = pl.BlockSpec(cat_block, lambda i: (i, 0, 0, 0))
    return pl.pallas_call(
        functools.partial(_norm_qkv_kernel, head_norm=head_norm),
        name="norm_qkv_hn" if head_norm else "norm_qkv",
        grid=(n // TOKEN_TILE,),
        in_specs=[
            pl.BlockSpec((TOKEN_TILE, D_MODEL), lambda i: (i, 0)),
            pl.BlockSpec((1, D_MODEL), const),
            _layer_weight(w, layer),
            pl.BlockSpec((1, LANES), const),
            pl.BlockSpec((1, LANES), const),
        ],
        out_specs=[pl.BlockSpec((TOKEN_TILE, D_MODEL), lambda i: (i, 0)), cat_spec, cat_spec],
        out_shape=[jax.ShapeDtypeStruct((n, D_MODEL), bf16), cat_shape, cat_shape],
        compiler_params=pltpu.CompilerParams(
            dimension_semantics=("parallel",), vmem_limit_bytes=VMEM_LIMIT),
    )(x, gain, w, q_gain, k_gain)


def _split_trunc(x):
    hi = pltpu.bitcast(pltpu.bitcast(x, jnp.uint32) & jnp.uint32(0xFFFF0000), f32)
    return hi.astype(bf16), (x - hi).astype(bf16)


def _sb_attn_kernel(q_ref, kcat_ref, vcat_ref, o_ref, lb_ref, cum_ref, rest_ref, acc_ref):
    t = SB_TILE
    g = pl.program_id(2)
    heads = range(HEADS_PER_TILE)

    row = lax.broadcasted_iota(jnp.int32, (t, 2 * t), 0)
    col = lax.broadcasted_iota(jnp.int32, (t, 2 * t), 1)
    causal = col % t < row
    rk = lax.broadcasted_iota(jnp.int32, (2 * t, 2 * t), 0) % t
    ck = lax.broadcasted_iota(jnp.int32, (2 * t, 2 * t), 1)
    cum_mat = jnp.where((rk > ck) | (ck >= t), 1.0, 0.0).astype(bf16)

    queries = [q_ref[0, sub * t:(sub + 1) * t, :] * ATTN_SCALE for sub in range(SB_STEP_TILES)]

    def raw_scores(sub, j):
        return _dot_nt(queries[sub], kcat_ref[j])

    def scores(z, diagonal):
        decay = jnp.maximum(z, 0.0) + jnp.log(1.0 + jnp.exp2(jnp.abs(z) * -LOG2_E))
        if diagonal:
            decay = jnp.where(causal, decay, 0.0)
        cums = []
        for head in heads:
            hi, lo = _split_trunc(decay[:, head * t:(head + 1) * t])
            cums.append(_dot(jnp.concatenate([hi, lo], axis=1), cum_mat))
        return z - decay, cums

    def weights(log_beta, cums, rests, j, diagonal):
        logs = [log_beta[:, head * t:(head + 1) * t] - cums[head][:, :t] for head in heads]
        if rests is None:
            rests = [cums[head][:, t:] for head in heads]
        else:
            logs = [logs[head] - rests[head] for head in heads]
            rests = [rests[head] + cums[head][:, t:] for head in heads]
        a = jnp.exp(jnp.concatenate(logs, axis=1))
        if diagonal:
            a = jnp.where(causal, a, 0.0)
        return rests, _dot(a.astype(bf16), vcat_ref[j])

    def first_sweep(tile_of, depth_of):
        slots = [(sub, n) for sub in range(SB_STEP_TILES) for n in range(depth_of(sub))]
        zz = {}
        rests, acc = None, None

        def stage_scores(u):
            sub, n = slots[u]
            zz[u] = raw_scores(sub, tile_of(sub) - n)

        def stage_cumsum(u):
            log_beta, cums = scores(zz.pop(u), slots[u][1] == 0)
            lb_ref[u] = log_beta
            for head in heads:
                cum_ref[u, head] = cums[head]

        def stage_output(u):
            nonlocal rests, acc
            sub, n = slots[u]
            rests, out = weights(lb_ref[u], [cum_ref[u, head] for head in heads],
                                 None if n == 0 else rests, tile_of(sub) - n, n == 0)
            acc = out if n == 0 else acc + out
            if n == depth_of(sub) - 1:
                for head in heads:
                    rest_ref[sub, head] = rests[head]
                acc_ref[sub] = acc

        for step in range(len(slots) + 2):
            if step < len(slots):
                stage_scores(step)
            if 1 <= step <= len(slots):
                stage_cumsum(step - 1)
            if step >= 2:
                stage_output(step - 2)

    @pl.when(g == 0)
    def _():
        first_sweep(lambda sub: sub, lambda sub: min(SB_FAST_TILES, sub + 1))

    @pl.when(g > 0)
    def _():
        first_sweep(lambda sub: g * SB_STEP_TILES + sub, lambda sub: SB_FAST_TILES)

    def least_decay(rests):
        return jnp.min(functools.reduce(jnp.minimum, rests))

    def tile_rests(sub):
        return [rest_ref[sub, head] for head in heads]

    @pl.when(least_decay([r for sub in range(SB_STEP_TILES) for r in tile_rests(sub)])
             < SB_DECAY_CUTOFF)
    def _():
        def cond(state):
            return functools.reduce(jnp.minimum, state[1:]) < SB_DECAY_CUTOFF

        def body(state):
            n, leasts = state[0], state[1:]
            updated = []
            for sub in range(SB_STEP_TILES):
                j = g * SB_STEP_TILES + sub - n

                def visit(sub=sub, j=j):
                    j = jnp.maximum(j, 0)
                    log_beta, cums = scores(raw_scores(sub, j), False)
                    rests, out = weights(log_beta, cums, tile_rests(sub), j, False)
                    for head in heads:
                        rest_ref[sub, head] = rests[head]
                    acc_ref[sub] = acc_ref[sub] + out
                    return least_decay(rests)

                def skip(j=j, least=leasts[sub]):
                    return jnp.where(j >= 0, least, SB_DECAY_DONE)

                pending = jnp.logical_and(j >= 0, leasts[sub] < SB_DECAY_CUTOFF)
                updated.append(lax.cond(pending, visit, skip))
            return (n + 1, *updated)

        lax.while_loop(cond, body, (jnp.int32(SB_FAST_TILES),
                                    *[least_decay(tile_rests(sub)) for sub in range(SB_STEP_TILES)]))

    for sub in range(SB_STEP_TILES):
        o_ref[0, sub * t:(sub + 1) * t, :] = acc_ref[sub].astype(o_ref.dtype)


def _attention_specs(b, s, step):
    rows = pl.BlockSpec((1, step, LANES), lambda bi, p, i: (bi, i, p))
    cat = pl.BlockSpec((None, s // KEY_TILE, None, HEADS_PER_TILE * KEY_TILE, LANES),
                       lambda bi, p, i: (bi, 0, p, 0, 0))
    return (b, N_PAIRS, s // step), [rows, cat, cat], rows


def _sb_attention(q, kcat, vcat):
    b, s, _ = q.shape
    t = SB_TILE
    slots = SB_STEP_TILES * SB_FAST_TILES
    grid, in_specs, out_spec = _attention_specs(b, s, SB_STEP_TILES * t)
    return pl.pallas_call(
        _sb_attn_kernel,
        name="sb_attention",
        grid=grid,
        in_specs=in_specs,
        out_specs=out_spec,
        out_shape=jax.ShapeDtypeStruct((b, s, D_MODEL), bf16),
        scratch_shapes=[pltpu.VMEM((slots, t, HEADS_PER_TILE * t), f32),
                        pltpu.VMEM((slots, HEADS_PER_TILE, t, 2 * t), f32),
                        pltpu.VMEM((SB_STEP_TILES, HEADS_PER_TILE, t, t), f32),
                        pltpu.VMEM((SB_STEP_TILES, t, LANES), f32)],
        compiler_params=pltpu.CompilerParams(
            dimension_semantics=("parallel", "parallel", "parallel"),
            vmem_limit_bytes=VMEM_LIMIT),
    )(q, kcat, vcat)


def _chunk_bias_kernel(u_ref, tab_ref):
    t = CA_TILE
    pair = pl.program_id(0)
    r = lax.broadcasted_iota(jnp.int32, (t, t), 0)
    c = lax.broadcasted_iota(jnp.int32, (t, t), 1)
    for head in range(HEADS_PER_TILE):
        u = jnp.broadcast_to(u_ref[pl.ds(pair * HEADS_PER_TILE + head, 1), :], (t, CA_ROLL_WIDTH))
        skew = pltpu.roll(u, 0, 1, stride=1, stride_axis=0)
        for kt in range(CA_KEY_TILES):
            chunk_dist = r // CHUNK + LEFT_CHUNKS - (kt * t + c) // CHUNK
            in_band = (chunk_dist >= 0) & (chunk_dist <= LEFT_CHUNKS)
            tab_ref[0, kt, :, head * t:(head + 1) * t] = jnp.where(
                in_band, skew[:, kt * t:(kt + 1) * t], MASK_VALUE)


def _chunk_bias_table(rel_bias):
    h = rel_bias.shape[0]
    far = rel_bias[:, 2 * MAX_REL:]
    near = rel_bias[:, 2 * MAX_REL - (CA_BAND - MAX_REL) + 1:][:, ::-1]
    u = jnp.concatenate([jnp.broadcast_to(far, (h, CA_BAND - 2 * MAX_REL + CA_TILE)), near,
                         jnp.broadcast_to(far, (h, CA_ROLL_WIDTH - CA_BAND))], axis=1)
    tile = (CA_KEY_TILES, CA_TILE, HEADS_PER_TILE * CA_TILE)
    return pl.pallas_call(
        _chunk_bias_kernel,
        name="chunk_bias_table",
        grid=(h // HEADS_PER_TILE,),
        in_specs=[pl.BlockSpec((h, CA_ROLL_WIDTH), lambda i: (0, 0))],
        out_specs=pl.BlockSpec((1,) + tile, lambda i: (i, 0, 0, 0)),
        out_shape=jax.ShapeDtypeStruct((h // HEADS_PER_TILE,) + tile, f32),
    )(u.astype(f32))


def _chunk_attn_kernel(q_ref, kcat_ref, vcat_ref, bias_ref, o_ref, logit_ref):
    t = CA_TILE
    g = pl.program_id(2)
    heads = range(HEADS_PER_TILE)
    r = lax.broadcasted_iota(jnp.int32, (HEADS_PER_TILE * t, LANES), 0) // t
    c = lax.broadcasted_iota(jnp.int32, (HEADS_PER_TILE * t, LANES), 1) // HEAD_DIM
    head_sums = jnp.where(r == c, 1.0, 0.0).astype(bf16)

    def sweep(tile_of):
        logits = {}

        def stage_logits(sub):
            q = q_ref[0, sub * t:(sub + 1) * t, :] * ATTN_SCALE
            tiles = []
            for kt in range(CA_KEY_TILES):
                j = tile_of(sub) - (CA_KEY_TILES - 1) + kt
                if isinstance(j, int) and j < 0:
                    continue
                logit_ref[sub % 2, kt] = _dot_nt(q, kcat_ref[j]) + bias_ref[0, kt]
                tiles.append((j, kt))
            logits[sub] = tiles

        def stage_output(sub):
            tiles = [(j, logit_ref[sub % 2, kt]) for j, kt in logits.pop(sub)]
            top = functools.reduce(jnp.maximum, [s for _, s in tiles])
            tops = [jnp.max(top[:, head * t:(head + 1) * t], axis=-1, keepdims=True)
                    for head in heads]
            acc = None
            for j, s in tiles:
                p = jnp.concatenate([jnp.exp(s[:, head * t:(head + 1) * t] - tops[head])
                                     for head in heads], axis=1)
                values = jnp.concatenate([vcat_ref[j], head_sums], axis=1)
                out = _dot(p.astype(bf16), values)
                acc = out if acc is None else acc + out
            o_ref[0, sub * t:(sub + 1) * t, :] = (acc[:, :LANES] / acc[:, LANES:]).astype(o_ref.dtype)

        for step in range(CA_STEP_TILES + 1):
            if step < CA_STEP_TILES:
                stage_logits(step)
            if step >= 1:
                stage_output(step - 1)

    @pl.when(g == 0)
    def _():
        sweep(lambda sub: sub)

    @pl.when(g > 0)
    def _():
        sweep(lambda sub: g * CA_STEP_TILES + sub)


def _chunk_attention(q, kcat, vcat, rel_bias):
    b, s, _ = q.shape
    t = CA_TILE
    table = _chunk_bias_table(rel_bias)
    grid, in_specs, out_spec = _attention_specs(b, s, CA_STEP_TILES * t)
    bias_spec = pl.BlockSpec((1, CA_KEY_TILES, t, HEADS_PER_TILE * t),
                             lambda bi, p, i: (p, 0, 0, 0))
    return pl.pallas_call(
        _chunk_attn_kernel,
        name="chunk_attention",
        grid=grid,
        in_specs=in_specs + [bias_spec],
        out_specs=out_spec,
        out_shape=jax.ShapeDtypeStruct((b, s, D_MODEL), bf16),
        scratch_shapes=[pltpu.VMEM((2, CA_KEY_TILES, t, HEADS_PER_TILE * t), f32)],
        compiler_params=pltpu.CompilerParams(
            dimension_semantics=("parallel", "parallel", "parallel"),
            vmem_limit_bytes=VMEM_LIMIT),
    )(q, kcat, vcat, table)


def _proj_mlp_kernel(x_ref, o_ref, wo_ref, g_ref, wup_ref, wdn_ref, out_ref):
    x = x_ref[...] + _dot(o_ref[...], wo_ref[...])
    h = (x * _rms_scale(x) * g_ref[...]).astype(bf16)
    acc = x
    for c in range(D_FF // FF_TILE):
        cols = slice(c * FF_TILE, (c + 1) * FF_TILE)
        u = jnp.square(jnp.maximum(_dot(h, wup_ref[:, cols]), 0.0))
        acc = acc + _dot(u.astype(bf16), wdn_ref[cols, :])
    out_ref[...] = acc


def _proj_mlp(x, o, wo, gain, wup, wdn, layer):
    n = x.shape[0]
    const = lambda i: (0, 0)
    rows = lambda i: (i, 0)
    return pl.pallas_call(
        _proj_mlp_kernel,
        name="proj_mlp",
        grid=(n // TOKEN_TILE,),
        in_specs=[
            pl.BlockSpec((TOKEN_TILE, D_MODEL), rows),
            pl.BlockSpec((TOKEN_TILE, D_MODEL), rows),
            _layer_weight(wo, layer),
            pl.BlockSpec((1, D_MODEL), const),
            _layer_weight(wup, layer),
            _layer_weight(wdn, layer),
        ],
        out_specs=pl.BlockSpec((TOKEN_TILE, D_MODEL), rows),
        out_shape=jax.ShapeDtypeStruct((n, D_MODEL), f32),
        compiler_params=pltpu.CompilerParams(
            dimension_semantics=("parallel",), vmem_limit_bytes=VMEM_LIMIT),
    )(x, o, wo, gain, wup, wdn)


def kernel(x, mix_norm, w_qkv, w_o, q_norm, k_norm, rel_bias, ffn_norm, w_up, w_down):
    b, s, d = x.shape
    depth = w_qkv.shape[0]
    x = x.reshape(b * s, d)
    unit_gain = jnp.ones((1, LANES), f32)
    w_qkv, w_o, w_up, w_down = (w.astype(bf16) for w in (w_qkv, w_o, w_up, w_down))
    for layer in range(depth):
        chunked = layer % 2 == 1
        idx = layer // 2
        if chunked:
            q_gain = jnp.tile(q_norm[idx], HEADS_PER_TILE)[None]
            k_gain = jnp.tile(k_norm[idx], HEADS_PER_TILE)[None]
        else:
            q_gain = k_gain = unit_gain
        q, kcat, vcat = _norm_qkv(x, mix_norm[layer][None], w_qkv, layer, q_gain, k_gain, chunked)
        q = q.reshape(b, s, d)
        kcat = kcat.reshape((b, s // KEY_TILE) + kcat.shape[1:])
        vcat = vcat.reshape((b, s // KEY_TILE) + vcat.shape[1:])
        if chunked:
            o = _chunk_attention(q, kcat, vcat, rel_bias[idx])
        else:
            o = _sb_attention(q, kcat, vcat)
        x = _proj_mlp(x, o.reshape(b * s, d), w_o, ffn_norm[layer][None], w_up, w_down, layer)
    return x.reshape(b, s, d)
```
